```python
import jax, jax.numpy as jnp
from jax import lax
import numpy as np

D_MODEL = 1024
BATCH = 8
SEQ = 2048
DEPTH = 2

SG_WIDTH = D_MODEL // 2
SG_CHUNK = 128
SG_GROUPS = 4
CV_WIDTH = D_MODEL // 2
CV_KERNEL = 31
HEAD_DIM = 64
N_Q_HEADS = D_MODEL // 2 // HEAD_DIM
N_KV_HEADS = N_Q_HEADS // 4
Q_WIDTH = N_Q_HEADS * HEAD_DIM
KV_WIDTH = N_KV_HEADS * HEAD_DIM
WINDOW = 128
ROPE_THETA = 10000.0
SC_WIDTH = D_MODEL // 2
SC_KERNEL = 3
N_BRANCH = 4
BRANCH_WIDTH = D_MODEL // 2
D_FF = -(-8 * D_MODEL // (3 * 256)) * 256
EPS = 1e-6

_PROJ_WIDTHS = (2 * SG_WIDTH, 2 * CV_WIDTH, Q_WIDTH, KV_WIDTH, KV_WIDTH, 3 * SC_WIDTH, N_BRANCH * D_MODEL)
PROJ_WIDTH = sum(_PROJ_WIDTHS)
PROJ_SPLITS = tuple(sum(_PROJ_WIDTHS[:i + 1]) for i in range(len(_PROJ_WIDTHS) - 1))

kernel_name = "hybrid_gated_four_mixer_block"


def rmsnorm(x, g):
    xf = x.astype(jnp.float32)
    y = xf * lax.rsqrt(jnp.mean(xf * xf, axis=-1, keepdims=True) + EPS)
    return (y * g.astype(jnp.float32)).astype(x.dtype)


def layernorm(x, g, b):
    xf = x.astype(jnp.float32)
    mu = jnp.mean(xf, axis=-1, keepdims=True)
    var = jnp.mean(jnp.square(xf - mu), axis=-1, keepdims=True)
    y = (xf - mu) * lax.rsqrt(var + EPS)
    return (y * g.astype(jnp.float32) + b.astype(jnp.float32)).astype(x.dtype)


def causal_depthwise_conv(x, w):
    k = w.shape[0]
    return lax.conv_general_dilated(
        x, w[:, None, :].astype(x.dtype), window_strides=(1,), padding=[(k - 1, 0)],
        dimension_numbers=('NWC', 'WIO', 'NWC'), feature_group_count=x.shape[-1])


def rotary(t, cos, sin):
    t1, t2 = jnp.split(t, 2, axis=-1)
    c = cos[None, :, None, :]
    s = sin[None, :, None, :]
    return jnp.concatenate([t1 * c - t2 * s, t2 * c + t1 * s], axis=-1)


def spatial_gating(z, ln_g, ln_b, w_s, b_s):
    u, v = jnp.split(z, 2, axis=-1)
    v = layernorm(v, ln_g, ln_b)
    b, s, _ = v.shape
    nc = s // SG_CHUNK
    vc = v.reshape(b, nc, SG_CHUNK, SG_GROUPS, SG_WIDTH // SG_GROUPS)
    tril = jnp.tril(jnp.ones((SG_CHUNK, SG_CHUNK), dtype=bool))
    w = jnp.where(tril[None], w_s, jnp.zeros((), w_s.dtype))
    mixed = jnp.einsum('gts,bnsgc->bntgc', w, vc) + b_s.T[None, None, :, :, None]
    return u * mixed.reshape(b, s, SG_WIDTH)


def conformer_conv(z, w_dw, b_dw, ln_g, ln_b):
    a, gate = jnp.split(z, 2, axis=-1)
    y = a * jax.nn.sigmoid(gate)
    y = causal_depthwise_conv(y, w_dw) + b_dw
    y = layernorm(y, ln_g, ln_b)
    return jax.nn.silu(y)


def short_gated_conv(z, w_sc):
    bg, cg, h = jnp.split(z, 3, axis=-1)
    return bg * causal_depthwise_conv(cg * h, w_sc)


def sliding_window_attention(q, k, v, sinks):
    b, s, _, _ = q.shape
    nb = s // WINDOW
    g = N_Q_HEADS // N_KV_HEADS
    qb = q.reshape(b, nb, WINDOW, N_KV_HEADS, g, HEAD_DIM)

    def band(t):
        tp = jnp.pad(t, ((0, 0), (WINDOW, 0), (0, 0), (0, 0)))
        prev = tp[:, :s].reshape(b, nb, WINDOW, N_KV_HEADS, HEAD_DIM)
        cur = t.reshape(b, nb, WINDOW, N_KV_HEADS, HEAD_DIM)
        return jnp.concatenate([prev, cur], axis=2)

    kb, vb = band(k), band(v)
    scores = jnp.einsum('bnqhgd,bnkhd->bnhgqk', qb, kb).astype(jnp.float32) * (HEAD_DIM ** -0.5)
    qi = jnp.arange(WINDOW)[None, :, None]
    kj = jnp.arange(2 * WINDOW)[None, None, :]
    blk = jnp.arange(nb)[:, None, None]
    delta = qi + WINDOW - kj
    valid = (delta >= 0) & (delta < WINDOW) & (blk * WINDOW + kj - WINDOW >= 0)
    scores = jnp.where(valid[None, :, None, None], scores, -jnp.inf)
    sink = sinks.astype(jnp.float32).reshape(N_KV_HEADS, g)[None, None, :, :, None, None]
    m = jnp.maximum(jnp.max(scores, axis=-1, keepdims=True), sink)
    p = jnp.exp(scores - m)
    probs = (p / (jnp.sum(p, axis=-1, keepdims=True) + jnp.exp(sink - m))).astype(v.dtype)
    out = jnp.einsum('bnhgqk,bnkhd->bnqhgd', probs, vb)
    return out.reshape(b, s, Q_WIDTH)


def setup_inputs(seed: int = 0) -> dict:
    key = jax.random.key(seed)
    ks = jax.random.split(key, 24)
    f32 = jnp.float32
    nrm = lambda k, shape, scale: jax.random.normal(k, shape, f32) * scale
    gain = lambda k, shape: 1.0 + 0.02 * jax.random.normal(k, shape, f32)
    return {
        "x": nrm(ks[0], (BATCH, SEQ, D_MODEL), 1.0),
        "norm_mix": gain(ks[1], (DEPTH, D_MODEL)),
        "w_in": nrm(ks[2], (DEPTH, D_MODEL, PROJ_WIDTH), D_MODEL ** -0.5),
        "sg_ln_g": gain(ks[3], (DEPTH, SG_WIDTH)),
        "sg_ln_b": nrm(ks[4], (DEPTH, SG_WIDTH), 0.02),
        "sg_w": nrm(ks[5], (DEPTH, SG_GROUPS, SG_CHUNK, SG_CHUNK), SG_CHUNK ** -0.5),
        "sg_b": 1.0 + nrm(ks[6], (DEPTH, SG_GROUPS, SG_CHUNK), 0.1),
        "cv_w": nrm(ks[7], (DEPTH, CV_KERNEL, CV_WIDTH), CV_KERNEL ** -0.5),
        "cv_b": nrm(ks[8], (DEPTH, CV_WIDTH), 0.02),
        "cv_ln_g": gain(ks[9], (DEPTH, CV_WIDTH)),
        "cv_ln_b": nrm(ks[10], (DEPTH, CV_WIDTH), 0.02),
        "attn_sinks": nrm(ks[11], (DEPTH, N_Q_HEADS), 1.0),
        "sc_w": nrm(ks[12], (DEPTH, SC_KERNEL, SC_WIDTH), SC_KERNEL ** -0.5),
        "w_branch": nrm(ks[13], (DEPTH, N_BRANCH, BRANCH_WIDTH, D_MODEL), BRANCH_WIDTH ** -0.5),
        "w_out": nrm(ks[14], (DEPTH, D_MODEL, D_MODEL), 0.5 * D_MODEL ** -0.5),
        "norm_ffn": gain(ks[15], (DEPTH, D_MODEL)),
        "w_gate_up": nrm(ks[16], (DEPTH, D_MODEL, 2 * D_FF), D_MODEL ** -0.5),
        "w_down": nrm(ks[17], (DEPTH, D_FF, D_MODEL), D_FF ** -0.5),
        "norm_final": gain(ks[18], (D_MODEL,)),
    }


def reference(x, norm_mix, w_in, sg_ln_g, sg_ln_b, sg_w, sg_b, cv_w, cv_b, cv_ln_g, cv_ln_b,
              attn_sinks, sc_w, w_branch, w_out, norm_ffn, w_gate_up, w_down, norm_final):
    b, s, _ = x.shape
    pos = jnp.arange(s, dtype=jnp.float32)
    inv_freq = 1.0 / (ROPE_THETA ** (jnp.arange(0, HEAD_DIM, 2, dtype=jnp.float32) / HEAD_DIM))
    ang = pos[:, None] * inv_freq[None, :]
    cos = jnp.cos(ang).astype(x.dtype)
    sin = jnp.sin(ang).astype(x.dtype)

    for l in range(DEPTH):
        xn = rmsnorm(x, norm_mix[l])
        proj = xn @ w_in[l]
        z_a, z_b, q, k, v, z_d, z_g = jnp.split(proj, PROJ_SPLITS, axis=-1)

        y_a = spatial_gating(jax.nn.gelu(z_a, approximate=False), sg_ln_g[l], sg_ln_b[l], sg_w[l], sg_b[l])
        y_b = conformer_conv(z_b, cv_w[l], cv_b[l], cv_ln_g[l], cv_ln_b[l])
        q = rotary(q.reshape(b, s, N_Q_HEADS, HEAD_DIM), cos, sin)
        k = rotary(k.reshape(b, s, N_KV_HEADS, HEAD_DIM), cos, sin)
        v = v.reshape(b, s, N_KV_HEADS, HEAD_DIM)
        y_c = sliding_window_attention(q, k, v, attn_sinks[l])
        y_d = short_gated_conv(z_d, sc_w[l])

        ys = jnp.stack([y_a, y_b, y_c, y_d], axis=0)
        branch = jnp.einsum('nbsc,ncd->nbsd', ys, w_branch[l])
        gates = jax.nn.sigmoid(z_g.reshape(b, s, N_BRANCH, D_MODEL))
        merged = jnp.einsum('bsnd,nbsd->bsd', gates, branch)
        x = x + merged @ w_out[l]

        hn = rmsnorm(x, norm_ffn[l])
        gate, up = jnp.split(hn @ w_gate_up[l], 2, axis=-1)
        x = x + (jax.nn.silu(gate) * up) @ w_down[l]

    return rmsnorm(x, norm_final)
```

```python
import functools
import math

import jax
import jax.numpy as jnp
from jax import lax
from jax.experimental import pallas as pl
from jax.experimental.pallas import tpu as pltpu

D_MODEL = 1024
HALF = D_MODEL // 2
SG_CHUNK = 128
SG_GROUPS = 4
CV_KERNEL = 31
HEAD_DIM = 64
N_Q_HEADS = 8
N_KV_HEADS = 2
Q_WIDTH = N_Q_HEADS * HEAD_DIM
KV_WIDTH = N_KV_HEADS * HEAD_DIM
WINDOW = 128
ROPE_THETA = 10000.0
SC_KERNEL = 3
N_BRANCH = 4
D_FF = 2816
EPS = 1e-6

OFF_A = 0
OFF_B = OFF_A + 2 * HALF
OFF_Q = OFF_B + 2 * HALF
OFF_K = OFF_Q + Q_WIDTH
OFF_V = OFF_K + KV_WIDTH
OFF_D = OFF_V + KV_WIDTH
OFF_G = OFF_D + 3 * HALF
PROJ_WIDTH = OFF_G + N_BRANCH * D_MODEL

LANES = 128
SUBLANES = 8
TILE = 256
CV_HALO = 32
SC_HALO = SUBLANES
MASK_VALUE = -1e30
VMEM_LIMIT_BYTES = 56 * 1024 * 1024

F32 = jnp.float32
BF16 = jnp.bfloat16


def _sigmoid(x):
    return 0.5 * jnp.tanh(0.5 * x) + 0.5


def _rmsnorm(x, g):
    return x * lax.rsqrt(jnp.mean(x * x, axis=-1, keepdims=True) + EPS) * g


def _layernorm(x, g, b):
    mu = jnp.mean(x, axis=-1, keepdims=True)
    xc = x - mu
    var = jnp.mean(xc * xc, axis=-1, keepdims=True)
    return xc * lax.rsqrt(var + EPS) * g + b


def _dot(a, b):
    return jnp.dot(a, b, preferred_element_type=F32)


def _dot_nt(a, b):
    return lax.dot_general(a, b, (((1,), (1,)), ((), ())), preferred_element_type=F32)


def _mixer_kernel(sink_ref, x_ref, cos_ref, sin_ref, nm_ref, w_in_ref,
                  sg_g_ref, sg_beta_ref, sg_w_ref, sg_bt_ref,
                  cv_w_ref, cv_b_ref, cv_g_ref, cv_beta_ref, sc_w_ref,
                  w_br_ref, w_out_ref, o_ref,
                  cv_buf, sc_buf, k_buf, v_buf, ya_s, yc_s):
    j = pl.program_id(1)

    @pl.when(j == 0)
    def _():
        cv_buf[0:CV_HALO, :] = jnp.zeros((CV_HALO, HALF), F32)
        sc_buf[0:SC_HALO, :] = jnp.zeros((SC_HALO, HALF), F32)
        k_buf[:, 0:WINDOW, :] = jnp.zeros((4, WINDOW, LANES), BF16)
        v_buf[:, 0:WINDOW, :] = jnp.zeros((4, WINDOW, LANES), BF16)

    x = x_ref[...]
    xn = _rmsnorm(x, nm_ref[...]).astype(BF16)

    def proj(off, width):
        return _dot(xn, w_in_ref[:, off:off + width])

    za = proj(OFF_A, 2 * HALF)
    ga = 0.5 * za * (1.0 + lax.erf(za * (1.0 / math.sqrt(2.0))))
    u = ga[:, :HALF]
    vln = _layernorm(ga[:, HALF:], sg_g_ref[...], sg_beta_ref[...]).astype(BF16)
    row = lax.broadcasted_iota(jnp.int32, (SG_CHUNK, SG_CHUNK), 0)
    col = lax.broadcasted_iota(jnp.int32, (SG_CHUNK, SG_CHUNK), 1)
    gw = SG_CHUNK
    for g in range(SG_GROUPS):
        w_g = jnp.where(row >= col, sg_w_ref[g], 0.0).astype(BF16)
        b_g = sg_bt_ref[:, g:g + 1]
        for c in range(TILE // SG_CHUNK):
            r0 = c * SG_CHUNK
            mixed = _dot(w_g, vln[r0:r0 + SG_CHUNK, g * gw:(g + 1) * gw]) + b_g
            ya_s[r0:r0 + SG_CHUNK, g * gw:(g + 1) * gw] = (
                u[r0:r0 + SG_CHUNK, g * gw:(g + 1) * gw] * mixed).astype(BF16)

    zb = proj(OFF_B, 2 * HALF)
    cv_buf[CV_HALO:CV_HALO + TILE, :] = zb[:, :HALF] * _sigmoid(zb[:, HALF:])
    base = CV_HALO - (CV_KERNEL - 1)
    acc = cv_w_ref[0:1, :] * cv_buf[base:base + TILE, :]
    for t in range(1, CV_KERNEL):
        acc = acc + cv_w_ref[t:t + 1, :] * cv_buf[base + t:base + t + TILE, :]
    cv_buf[0:CV_HALO, :] = cv_buf[TILE:TILE + CV_HALO, :]
    yb = _layernorm(acc + cv_b_ref[...], cv_g_ref[...], cv_beta_ref[...])
    yb = (yb * _sigmoid(yb)).astype(BF16)

    cos = cos_ref[...]
    sin = sin_ref[...]
    lane = lax.broadcasted_iota(jnp.int32, (TILE, LANES), 1)
    first_half = (lane % HEAD_DIM) < (HEAD_DIM // 2)
    low_head = lane < HEAD_DIM

    def rotary(t):
        swapped = jnp.where(first_half,
                            pltpu.roll(t, LANES - HEAD_DIM // 2, 1),
                            pltpu.roll(t, HEAD_DIM // 2, 1))
        return t * cos + swapped * sin

    kv = proj(OFF_K, 2 * KV_WIDTH)
    kr = rotary(kv[:, :KV_WIDTH])
    vv = kv[:, KV_WIDTH:]
    for idx, t in ((0, kr), (1, vv)):
        buf = k_buf if idx == 0 else v_buf
        sw = pltpu.roll(t, HEAD_DIM, 1)
        buf[0, WINDOW:WINDOW + TILE, :] = jnp.where(low_head, t, 0.0).astype(BF16)
        buf[1, WINDOW:WINDOW + TILE, :] = jnp.where(low_head, 0.0, sw).astype(BF16)
        buf[2, WINDOW:WINDOW + TILE, :] = jnp.where(low_head, sw, 0.0).astype(BF16)
        buf[3, WINDOW:WINDOW + TILE, :] = jnp.where(low_head, 0.0, t).astype(BF16)

    qz = proj(OFF_Q, Q_WIDTH)
    qi = lax.broadcasted_iota(jnp.int32, (WINDOW, 2 * WINDOW), 0)
    kj = lax.broadcasted_iota(jnp.int32, (WINDOW, 2 * WINDOW), 1)
    band = (kj > qi) & (kj <= qi + WINDOW)
    first_lo = jnp.where(j == 0, WINDOW, 0)
    low_head_w = lax.broadcasted_iota(jnp.int32, (WINDOW, LANES), 1) < HEAD_DIM
    for g in range(Q_WIDTH // LANES):
        h = (g * 2) // (N_Q_HEADS // N_KV_HEADS)
        qg = (rotary(qz[:, g * LANES:(g + 1) * LANES]) * (HEAD_DIM ** -0.5)).astype(BF16)
        for c in range(TILE // WINDOW):
            r0 = c * WINDOW
            valid = band & (kj >= first_lo) if c == 0 else band
            bias = jnp.where(valid, 0.0, MASK_VALUE)
            qb = qg[r0:r0 + WINDOW, :]
            out = None
            inv = []
            for side in range(2):
                s = _dot_nt(qb, k_buf[2 * h + side, r0:r0 + 2 * WINDOW, :]) + bias
                sink = sink_ref[2 * g + side]
                m = jnp.maximum(jnp.max(s, axis=-1, keepdims=True), sink)
                p = jnp.exp(s - m)
                den = jnp.sum(p, axis=-1, keepdims=True) + jnp.exp(sink - m)
                inv.append(1.0 / den)
                pv = _dot(p.astype(BF16), v_buf[2 * h + side, r0:r0 + 2 * WINDOW, :])
                out = pv if out is None else out + pv
            out = out * jnp.where(low_head_w, inv[0], inv[1])
            yc_s[r0:r0 + WINDOW, g * LANES:(g + 1) * LANES] = out.astype(BF16)
    for buf in (k_buf, v_buf):
        buf[:, 0:WINDOW, :] = buf[:, TILE:TILE + WINDOW, :]

    zd = proj(OFF_D, 3 * HALF)
    sc_buf[SC_HALO:SC_HALO + TILE, :] = zd[:, HALF:2 * HALF] * zd[:, 2 * HALF:]
    base = SC_HALO - (SC_KERNEL - 1)
    acc = sc_w_ref[0:1, :] * sc_buf[base:base + TILE, :]
    for t in range(1, SC_KERNEL):
        acc = acc + sc_w_ref[t:t + 1, :] * sc_buf[base + t:base + t + TILE, :]
    sc_buf[0:SC_HALO, :] = sc_buf[TILE:TILE + SC_HALO, :]
    yd = (zd[:, :HALF] * acc).astype(BF16)

    merged = None
    for n, y in enumerate((ya_s[...], yb, yc_s[...], yd)):
        gate = _sigmoid(proj(OFF_G + n * D_MODEL, D_MODEL))
        term = gate * _dot(y, w_br_ref[n])
        merged = term if merged is None else merged + term
    o_ref[...] = x + _dot(merged.astype(BF16), w_out_ref[...])


def _ffn_kernel(x_ref, nf_ref, w_gu_ref, w_dn_ref, nfinal_ref, o_ref, *, final_norm):
    x = x_ref[...]
    hn = _rmsnorm(x, nf_ref[...]).astype(BF16)
    gate = _dot(hn, w_gu_ref[:, :D_FF])
    up = _dot(hn, w_gu_ref[:, D_FF:])
    h = (gate * _sigmoid(gate) * up).astype(BF16)
    y = x + _dot(h, w_dn_ref[...])
    if final_norm:
        y = _rmsnorm(y, nfinal_ref[...])
    o_ref[...] = y


def _resident(shape):
    return pl.BlockSpec(shape, lambda b, j: (0,) * len(shape), pipeline_mode=pl.Buffered(1))


def _mixer_call(x, cos_t, sin_t, sinks, nm, w_in, sg_g, sg_beta, sg_w, sg_bt,
                cv_w, cv_b, cv_g, cv_beta, sc_w, w_br, w_out):
    batch, seq, _ = x.shape
    tile_spec = pl.BlockSpec((None, TILE, D_MODEL), lambda b, j: (b, j, 0))
    rope_spec = pl.BlockSpec((TILE, LANES), lambda b, j: (j, 0))
    in_specs = [
        pl.BlockSpec(memory_space=pltpu.SMEM),
        tile_spec, rope_spec, rope_spec,
        _resident((1, D_MODEL)),
        _resident((D_MODEL, PROJ_WIDTH)),
        _resident((1, HALF)), _resident((1, HALF)),
        _resident((SG_GROUPS, SG_CHUNK, SG_CHUNK)), _resident((SG_CHUNK, SG_GROUPS)),
        _resident((CV_KERNEL, HALF)), _resident((1, HALF)), _resident((1, HALF)), _resident((1, HALF)),
        _resident((SC_KERNEL, HALF)),
        _resident((N_BRANCH, HALF, D_MODEL)),
        _resident((D_MODEL, D_MODEL)),
    ]
    scratch = [
        pltpu.VMEM((CV_HALO + TILE, HALF), F32),
        pltpu.VMEM((SC_HALO + TILE, HALF), F32),
        pltpu.VMEM((4, WINDOW + TILE, LANES), BF16),
        pltpu.VMEM((4, WINDOW + TILE, LANES), BF16),
        pltpu.VMEM((TILE, HALF), BF16),
        pltpu.VMEM((TILE, HALF), BF16),
    ]
    return pl.pallas_call(
        _mixer_kernel,
        grid=(batch, seq // TILE),
        in_specs=in_specs,
        out_specs=tile_spec,
        out_shape=jax.ShapeDtypeStruct(x.shape, F32),
        scratch_shapes=scratch,
        compiler_params=pltpu.CompilerParams(
            dimension_semantics=("arbitrary", "arbitrary"),
            vmem_limit_bytes=VMEM_LIMIT_BYTES),
        name="mixer",
    )(sinks, x, cos_t, sin_t, nm, w_in, sg_g, sg_beta, sg_w, sg_bt,
      cv_w, cv_b, cv_g, cv_beta, sc_w, w_br, w_out)


def _ffn_call(x, nf, w_gu, w_dn, nfinal, final_norm):
    batch, seq, _ = x.shape
    tile_spec = pl.BlockSpec((None, TILE, D_MODEL), lambda b, j: (b, j, 0))
    return pl.pallas_call(
        functools.partial(_ffn_kernel, final_norm=final_norm),
        grid=(batch, seq // TILE),
        in_specs=[tile_spec, _resident((1, D_MODEL)), _resident((D_MODEL, 2 * D_FF)),
                  _resident((D_FF, D_MODEL)), _resident((1, D_MODEL))],
        out_specs=tile_spec,
        out_shape=jax.ShapeDtypeStruct(x.shape, F32),
        compiler_params=pltpu.CompilerParams(
            dimension_semantics=("arbitrary", "arbitrary"),
            vmem_limit_bytes=VMEM_LIMIT_BYTES),
        name="ffn",
    )(x, nf, w_gu, w_dn, nfinal)


def _rope_tables(seq):
    pos = jnp.arange(seq, dtype=F32)
    inv_freq = 1.0 / (ROPE_THETA ** (jnp.arange(0, HEAD_DIM, 2, dtype=F32) / HEAD_DIM))
    ang = pos[:, None] * inv_freq[None, :]
    cos = jnp.cos(ang)
    sin = jnp.sin(ang)
    reps = LANES // HEAD_DIM
    cos_t = jnp.tile(jnp.concatenate([cos, cos], axis=-1), (1, reps))
    sin_t = jnp.tile(jnp.concatenate([-sin, sin], axis=-1), (1, reps))
    return cos_t, sin_t


def kernel(x, norm_mix, w_in, sg_ln_g, sg_ln_b, sg_w, sg_b, cv_w, cv_b, cv_ln_g, cv_ln_b,
           attn_sinks, sc_w, w_branch, w_out, norm_ffn, w_gate_up, w_down, norm_final):
    depth = w_in.shape[0]
    seq = x.shape[1]
    assert x.shape[2] == D_MODEL and seq % TILE == 0 and TILE % WINDOW == 0
    cos_t, sin_t = _rope_tables(seq)
    row = lambda a: a.reshape(1, -1)
    for l in range(depth):
        x = _mixer_call(
            x, cos_t, sin_t, attn_sinks[l], row(norm_mix[l]), w_in[l].astype(BF16),
            row(sg_ln_g[l]), row(sg_ln_b[l]), sg_w[l], sg_b[l].T,
            cv_w[l], row(cv_b[l]), row(cv_ln_g[l]), row(cv_ln_b[l]), sc_w[l],
            w_branch[l].astype(BF16), w_out[l].astype(BF16))
        x = _ffn_call(x, row(norm_ffn[l]), w_gate_up[l].astype(BF16), w_down[l].astype(BF16),
                      row(norm_final), final_norm=(l == depth - 1))
    return x
```

```python
import functools
import math

import jax
import jax.numpy as jnp
from jax import lax
from jax.experimental import pallas as pl
from jax.experimental.pallas import tpu as pltpu

D_MODEL = 1024
HALF = D_MODEL // 2
SG_CHUNK = 128
SG_GROUPS = 4
CV_KERNEL = 31
HEAD_DIM = 64
N_Q_HEADS = 8
N_KV_HEADS = 2
Q_WIDTH = N_Q_HEADS * HEAD_DIM
KV_WIDTH = N_KV_HEADS * HEAD_DIM
WINDOW = 128
ROPE_THETA = 10000.0
SC_KERNEL = 3
N_BRANCH = 4
D_FF = 2816
EPS = 1e-6

OFF_A = 0
OFF_B = OFF_A + 2 * HALF
OFF_Q = OFF_B + 2 * HALF
OFF_K = OFF_Q + Q_WIDTH
OFF_V = OFF_K + KV_WIDTH
OFF_D = OFF_V + KV_WIDTH
OFF_G = OFF_D + 3 * HALF
PROJ_WIDTH = OFF_G + N_BRANCH * D_MODEL

LANES = 128
SUBLANES = 8
TILE = 256
CV_HALO = 32
SC_HALO = SUBLANES
CONV_ROWS = 128
MASK_VALUE = -1e30
VMEM_LIMIT_BYTES = 56 * 1024 * 1024

F32 = jnp.float32
BF16 = jnp.bfloat16


def _sigmoid(x):
    return 0.5 * jnp.tanh(0.5 * x) + 0.5


def _rmsnorm(x, g):
    return x * lax.rsqrt(jnp.mean(x * x, axis=-1, keepdims=True) + EPS) * g


def _layernorm(x, g, b):
    mu = jnp.mean(x, axis=-1, keepdims=True)
    xc = x - mu
    var = jnp.mean(xc * xc, axis=-1, keepdims=True)
    return xc * lax.rsqrt(var + EPS) * g + b


def _dot(a, b):
    return jnp.dot(a, b, preferred_element_type=F32)


def _dot_nt(a, b):
    return lax.dot_general(a, b, (((1,), (1,)), ((), ())), preferred_element_type=F32)


def _zero_after(t):
    return t[-1:, -LANES:] * 0.0


def _causal_conv_block(buf, w_ref, ksize, halo, r0, l0, after=None):
    rows = CONV_ROWS + SUBLANES
    out = None
    for r in range(min(SUBLANES, ksize)):
        z = None
        for a in range((ksize - 1 - r) // SUBLANES + 1):
            tap = ksize - 1 - (SUBLANES * a + r)
            start = halo + r0 - SUBLANES * (a + 1)
            w_tap = w_ref[tap:tap + 1, l0:l0 + LANES]
            if after is not None:
                w_tap = w_tap + after
            term = w_tap * buf[start:start + rows, l0:l0 + LANES]
            z = term if z is None else z + term
        if r:
            z = pltpu.roll(z, r, 0)
        z = z[SUBLANES:, :]
        out = z if out is None else out + z
    return out


def _mixer_kernel(sink_ref, x_ref, cos_ref, sin_ref, nm_ref, w_in_ref,
                  sg_g_ref, sg_beta_ref, sg_w_ref, sg_bt_ref,
                  cv_w_ref, cv_b_ref, cv_g_ref, cv_beta_ref, sc_w_ref,
                  w_br_ref, w_out_ref, o_ref,
                  cv_buf, sc_buf, k_buf, v_buf, ya_s, yc_s):
    j = pl.program_id(1)

    @pl.when(j == 0)
    def _():
        cv_buf[0:CV_HALO, :] = jnp.zeros((CV_HALO, HALF), F32)
        sc_buf[0:SC_HALO, :] = jnp.zeros((SC_HALO, HALF), F32)
        k_buf[:, 0:WINDOW, :] = jnp.zeros((4, WINDOW, LANES), BF16)
        v_buf[:, 0:WINDOW, :] = jnp.zeros((4, WINDOW, LANES), BF16)

    x = x_ref[...]
    xn_f32 = _rmsnorm(x, nm_ref[...])
    xn = xn_f32.astype(BF16)
    xn_half = (0.5 * xn_f32).astype(BF16)

    def proj(off, width):
        return _dot(xn, w_in_ref[:, off:off + width])

    zb = proj(OFF_B, 2 * HALF)
    cv_buf[CV_HALO:CV_HALO + TILE, :] = zb[:, :HALF] * _sigmoid(zb[:, HALF:])

    cos = cos_ref[...]
    sin = sin_ref[...]
    lane = lax.broadcasted_iota(jnp.int32, (TILE, LANES), 1)
    first_half = (lane % HEAD_DIM) < (HEAD_DIM // 2)
    low_head = lane < HEAD_DIM

    def rotary(t):
        swapped = jnp.where(first_half,
                            pltpu.roll(t, LANES - HEAD_DIM // 2, 1),
                            pltpu.roll(t, HEAD_DIM // 2, 1))
        return t * cos + swapped * sin

    kv = proj(OFF_K, 2 * KV_WIDTH)
    kr = rotary(kv[:, :KV_WIDTH])
    vv = kv[:, KV_WIDTH:]
    for idx, t in ((0, kr), (1, vv)):
        buf = k_buf if idx == 0 else v_buf
        sw = pltpu.roll(t, HEAD_DIM, 1)
        buf[0, WINDOW:WINDOW + TILE, :] = jnp.where(low_head, t, 0.0).astype(BF16)
        buf[1, WINDOW:WINDOW + TILE, :] = jnp.where(low_head, 0.0, sw).astype(BF16)
        buf[2, WINDOW:WINDOW + TILE, :] = jnp.where(low_head, sw, 0.0).astype(BF16)
        buf[3, WINDOW:WINDOW + TILE, :] = jnp.where(low_head, 0.0, t).astype(BF16)

    qz = proj(OFF_Q, Q_WIDTH)
    qi = lax.broadcasted_iota(jnp.int32, (WINDOW, 2 * WINDOW), 0)
    kj = lax.broadcasted_iota(jnp.int32, (WINDOW, 2 * WINDOW), 1)
    band = (kj > qi) & (kj <= qi + WINDOW)
    first_lo = jnp.where(j == 0, WINDOW, 0)
    bias_rest = jnp.where(band, 0.0, MASK_VALUE)
    bias_first = jnp.where(band & (kj >= first_lo), 0.0, MASK_VALUE)
    low_head_w = lax.broadcasted_iota(jnp.int32, (WINDOW, LANES), 1) < HEAD_DIM

    def attention_group(g, after):
        h = (g * 2) // (N_Q_HEADS // N_KV_HEADS)
        qg = rotary(qz[:, g * LANES:(g + 1) * LANES]) * (HEAD_DIM ** -0.5)
        if after is not None:
            qg = qg + after
        qg = qg.astype(BF16)
        for c in range(TILE // WINDOW):
            r0 = c * WINDOW
            bias = bias_first if c == 0 else bias_rest
            qb = qg[r0:r0 + WINDOW, :]
            out = None
            inv = []
            for side in range(2):
                s = _dot_nt(qb, k_buf[2 * h + side, r0:r0 + 2 * WINDOW, :]) + bias
                sink = sink_ref[2 * g + side]
                m = jnp.maximum(jnp.max(s, axis=-1, keepdims=True), sink)
                p = jnp.exp(s - m)
                den = jnp.sum(p, axis=-1, keepdims=True) + jnp.exp(sink - m)
                inv.append(1.0 / den)
                pv = _dot(p.astype(BF16), v_buf[2 * h + side, r0:r0 + 2 * WINDOW, :])
                out = pv if out is None else out + pv
            out = out * jnp.where(low_head_w, inv[0], inv[1])
            yc_s[r0:r0 + WINDOW, g * LANES:(g + 1) * LANES] = out.astype(BF16)

    assert HALF // LANES == N_BRANCH and Q_WIDTH // LANES == N_BRANCH
    conv_cols = []
    gates = []
    after = None
    for n in range(N_BRANCH):
        half_z = _dot(xn_half, w_in_ref[:, OFF_G + n * D_MODEL:OFF_G + (n + 1) * D_MODEL])
        t = jnp.tanh(half_z)
        gates.append(t + 1.0)
        conv_cols.append(jnp.concatenate(
            [_causal_conv_block(cv_buf, cv_w_ref, CV_KERNEL, CV_HALO, r0, n * LANES, after)
             for r0 in range(0, TILE, CONV_ROWS)], axis=0))
        attention_group(n, after)
        after = _zero_after(t)
    for buf in (k_buf, v_buf):
        buf[:, 0:WINDOW, :] = buf[:, TILE:TILE + WINDOW, :]
    conv = jnp.concatenate(conv_cols, axis=1)
    cv_buf[0:CV_HALO, :] = cv_buf[TILE:TILE + CV_HALO, :]

    za = proj(OFF_A, 2 * HALF)
    yb = _layernorm(conv + (cv_b_ref[...] + after[:, 0:1]), cv_g_ref[...], cv_beta_ref[...])
    yb = (yb * _sigmoid(yb)).astype(BF16)

    zd = proj(OFF_D, 3 * HALF)
    ga = 0.5 * za * (1.0 + lax.erf(za * (1.0 / math.sqrt(2.0))))
    u = ga[:, :HALF]
    vln = _layernorm(ga[:, HALF:], sg_g_ref[...], sg_beta_ref[...]).astype(BF16)

    row = lax.broadcasted_iota(jnp.int32, (SG_CHUNK, SG_CHUNK), 0)
    col = lax.broadcasted_iota(jnp.int32, (SG_CHUNK, SG_CHUNK), 1)
    gw = SG_CHUNK
    for g in range(SG_GROUPS):
        w_g = jnp.where(row >= col, sg_w_ref[g], 0.0).astype(BF16)
        b_g = sg_bt_ref[:, g:g + 1]
        for c in range(TILE // SG_CHUNK):
            r0 = c * SG_CHUNK
            mixed = _dot(w_g, vln[r0:r0 + SG_CHUNK, g * gw:(g + 1) * gw]) + b_g
            ya_s[r0:r0 + SG_CHUNK, g * gw:(g + 1) * gw] = (
                u[r0:r0 + SG_CHUNK, g * gw:(g + 1) * gw] * mixed).astype(BF16)

    sc_buf[SC_HALO:SC_HALO + TILE, :] = zd[:, HALF:2 * HALF] * zd[:, 2 * HALF:]
    conv = jnp.concatenate(
        [jnp.concatenate([_causal_conv_block(sc_buf, sc_w_ref, SC_KERNEL, SC_HALO, r0, l0)
                          for r0 in range(0, TILE, CONV_ROWS)], axis=0)
         for l0 in range(0, HALF, LANES)], axis=1)
    sc_buf[0:SC_HALO, :] = sc_buf[TILE:TILE + SC_HALO, :]
    yd = (zd[:, :HALF] * conv).astype(BF16)

    merged = (gates[1] * _dot(yb, w_br_ref[1]) + gates[2] * _dot(yc_s[...], w_br_ref[2])
              + gates[0] * _dot(ya_s[...], w_br_ref[0]) + gates[3] * _dot(yd, w_br_ref[3]))
    o_ref[...] = x + 0.5 * _dot(merged.astype(BF16), w_out_ref[...])


def _ffn_kernel(x_ref, nf_ref, w_gu_ref, w_dn_ref, nfinal_ref, o_ref, *, final_norm):
    x = x_ref[...]
    hn = _rmsnorm(x, nf_ref[...]).astype(BF16)
    gate = _dot(hn, w_gu_ref[:, :D_FF])
    up = _dot(hn, w_gu_ref[:, D_FF:])
    h = (gate * _sigmoid(gate) * up).astype(BF16)
    y = x + _dot(h, w_dn_ref[...])
    if final_norm:
        y = _rmsnorm(y, nfinal_ref[...])
    o_ref[...] = y


def _resident(shape, layer=None):
    if layer is None:
        return pl.BlockSpec(shape, lambda b, j: (0,) * len(shape), pipeline_mode=pl.Buffered(1))
    return pl.BlockSpec((None,) + shape, lambda b, j: (layer,) + (0,) * len(shape),
                        pipeline_mode=pl.Buffered(1))


def _mixer_call(layer, x, cos_t, sin_t, sinks, nm, w_in, sg_g, sg_beta, sg_w, sg_bt,
                cv_w, cv_b, cv_g, cv_beta, sc_w, w_br, w_out):
    batch, seq, _ = x.shape
    tile_spec = pl.BlockSpec((None, TILE, D_MODEL), lambda b, j: (b, j, 0))
    rope_spec = pl.BlockSpec((TILE, LANES), lambda b, j: (j, 0))
    in_specs = [
        pl.BlockSpec(memory_space=pltpu.SMEM),
        tile_spec, rope_spec, rope_spec,
        _resident((1, D_MODEL)),
        _resident((D_MODEL, PROJ_WIDTH), layer),
        _resident((1, HALF)), _resident((1, HALF)),
        _resident((SG_GROUPS, SG_CHUNK, SG_CHUNK)), _resident((SG_CHUNK, SG_GROUPS)),
        _resident((CV_KERNEL, HALF)), _resident((1, HALF)), _resident((1, HALF)), _resident((1, HALF)),
        _resident((SC_KERNEL, HALF)),
        _resident((N_BRANCH, HALF, D_MODEL), layer),
        _resident((D_MODEL, D_MODEL), layer),
    ]
    scratch = [
        pltpu.VMEM((CV_HALO + TILE, HALF), F32),
        pltpu.VMEM((SC_HALO + TILE, HALF), F32),
        pltpu.VMEM((4, WINDOW + TILE, LANES), BF16),
        pltpu.VMEM((4, WINDOW + TILE, LANES), BF16),
        pltpu.VMEM((TILE, HALF), BF16),
        pltpu.VMEM((TILE, HALF), BF16),
    ]
    return pl.pallas_call(
        _mixer_kernel,
        grid=(batch, seq // TILE),
        in_specs=in_specs,
        out_specs=tile_spec,
        out_shape=jax.ShapeDtypeStruct(x.shape, F32),
        scratch_shapes=scratch,
        compiler_params=pltpu.CompilerParams(
            dimension_semantics=("arbitrary", "arbitrary"),
            vmem_limit_bytes=VMEM_LIMIT_BYTES),
        name="mixer",
    )(sinks, x, cos_t, sin_t, nm, w_in, sg_g, sg_beta, sg_w, sg_bt,
      cv_w, cv_b, cv_g, cv_beta, sc_w, w_br, w_out)


def _ffn_call(layer, x, nf, w_gu, w_dn, nfinal, final_norm):
    batch, seq, _ = x.shape
    tile_spec = pl.BlockSpec((None, TILE, D_MODEL), lambda b, j: (b, j, 0))
    return pl.pallas_call(
        functools.partial(_ffn_kernel, final_norm=final_norm),
        grid=(batch, seq // TILE),
        in_specs=[tile_spec, _resident((1, D_MODEL)), _resident((D_MODEL, 2 * D_FF), layer),
                  _resident((D_FF, D_MODEL), layer), _resident((1, D_MODEL))],
        out_specs=tile_spec,
        out_shape=jax.ShapeDtypeStruct(x.shape, F32),
        compiler_params=pltpu.CompilerParams(
            dimension_semantics=("arbitrary", "arbitrary"),
            vmem_limit_bytes=VMEM_LIMIT_BYTES),
        name="ffn",
    )(x, nf, w_gu, w_dn, nfinal)


def _rope_tables(seq):
    pos = jnp.arange(seq, dtype=F32)
    inv_freq = 1.0 / (ROPE_THETA ** (jnp.arange(0, HEAD_DIM, 2, dtype=F32) / HEAD_DIM))
    ang = pos[:, None] * inv_freq[None, :]
    cos = jnp.cos(ang)
    sin = jnp.sin(ang)
    reps = LANES // HEAD_DIM
    cos_t = jnp.tile(jnp.concatenate([cos, cos], axis=-1), (1, reps))
    sin_t = jnp.tile(jnp.concatenate([-sin, sin], axis=-1), (1, reps))
    return cos_t, sin_t


def kernel(x, norm_mix, w_in, sg_ln_g, sg_ln_b, sg_w, sg_b, cv_w, cv_b, cv_ln_g, cv_ln_b,
           attn_sinks, sc_w, w_branch, w_out, norm_ffn, w_gate_up, w_down, norm_final):
    depth = w_in.shape[0]
    seq = x.shape[1]
    assert x.shape[2] == D_MODEL and seq % TILE == 0 and TILE % WINDOW == 0
    cos_t, sin_t = _rope_tables(seq)
    row = lambda a: a.reshape(1, -1)
    w_in, w_branch, w_out, w_gate_up, w_down = (
        w.astype(BF16) for w in (w_in, w_branch, w_out, w_gate_up, w_down))
    for l in range(depth):
        x = _mixer_call(
            l, x, cos_t, sin_t, attn_sinks[l], row(norm_mix[l]), w_in,
            row(sg_ln_g[l]), row(sg_ln_b[l]), sg_w[l], sg_b[l].T,
            cv_w[l], row(cv_b[l]), row(cv_ln_g[l]), row(cv_ln_b[l]), sc_w[l],
            w_branch, w_out)
        x = _ffn_call(l, x, row(norm_ffn[l]), w_gate_up, w_down,
                      row(norm_final), final_norm=(l == depth - 1))
    return x
```

```python
import functools
import math

import jax
import jax.numpy as jnp
from jax import lax
from jax.experimental import pallas as pl
from jax.experimental.pallas import tpu as pltpu

D_MODEL = 1024
HALF = D_MODEL // 2
SG_CHUNK = 128
SG_GROUPS = 4
CV_KERNEL = 31
HEAD_DIM = 64
N_Q_HEADS = 8
N_KV_HEADS = 2
Q_WIDTH = N_Q_HEADS * HEAD_DIM
KV_WIDTH = N_KV_HEADS * HEAD_DIM
WINDOW = 128
ROPE_THETA = 10000.0
SC_KERNEL = 3
N_BRANCH = 4
D_FF = 2816
EPS = 1e-6

OFF_A = 0
OFF_B = OFF_A + 2 * HALF
OFF_Q = OFF_B + 2 * HALF
OFF_K = OFF_Q + Q_WIDTH
OFF_V = OFF_K + KV_WIDTH
OFF_D = OFF_V + KV_WIDTH
OFF_G = OFF_D + 3 * HALF
PROJ_WIDTH = OFF_G + N_BRANCH * D_MODEL

LANES = 128
SUBLANES = 8
TILE = 512
CV_HALO = 32
SC_HALO = SUBLANES
CONV_ROWS = 128
MASK_VALUE = -1e30
VMEM_LIMIT_BYTES = 60 * 1024 * 1024

F32 = jnp.float32
BF16 = jnp.bfloat16


def _sigmoid(x):
    return 0.5 * jnp.tanh(0.5 * x) + 0.5


def _rmsnorm(x, g):
    return x * lax.rsqrt(jnp.mean(x * x, axis=-1, keepdims=True) + EPS) * g


def _layernorm(x, g, b):
    mu = jnp.mean(x, axis=-1, keepdims=True)
    xc = x - mu
    var = jnp.mean(xc * xc, axis=-1, keepdims=True)
    return xc * lax.rsqrt(var + EPS) * g + b


def _dot(a, b):
    return jnp.dot(a, b, preferred_element_type=F32)


def _dot_nt(a, b):
    return lax.dot_general(a, b, (((1,), (1,)), ((), ())), preferred_element_type=F32)


def _zero_after(t):
    return t[-1:, -LANES:] * 0.0


def _causal_conv_block(buf, w_ref, ksize, halo, r0, l0, after=None):
    rows = CONV_ROWS + SUBLANES
    out = None
    for r in range(min(SUBLANES, ksize)):
        z = None
        for a in range((ksize - 1 - r) // SUBLANES + 1):
            tap = ksize - 1 - (SUBLANES * a + r)
            start = halo + r0 - SUBLANES * (a + 1)
            w_tap = w_ref[tap:tap + 1, l0:l0 + LANES]
            if after is not None:
                w_tap = w_tap + after
            term = w_tap * buf[start:start + rows, l0:l0 + LANES]
            z = term if z is None else z + term
        if r:
            z = pltpu.roll(z, r, 0)
        z = z[SUBLANES:, :]
        out = z if out is None else out + z
    return out


def _mixer_kernel(sink_ref, x_ref, cos_ref, sin_ref, nm_ref, w_in_ref,
                  sg_g_ref, sg_beta_ref, sg_w_ref, sg_bt_ref,
                  cv_w_ref, cv_b_ref, cv_g_ref, cv_beta_ref, sc_w_ref,
                  w_br_ref, w_out_ref, o_ref,
                  cv_buf, sc_buf, k_buf, v_buf, ya_s, yc_s):
    j = pl.program_id(1)

    @pl.when(j == 0)
    def _():
        cv_buf[0:CV_HALO, :] = jnp.zeros((CV_HALO, HALF), F32)
        sc_buf[0:SC_HALO, :] = jnp.zeros((SC_HALO, HALF), F32)
        k_buf[:, 0:WINDOW, :] = jnp.zeros((4, WINDOW, LANES), BF16)
        v_buf[:, 0:WINDOW, :] = jnp.zeros((4, WINDOW, LANES), BF16)

    x = x_ref[...]
    xn_f32 = _rmsnorm(x, nm_ref[...])
    xn = xn_f32.astype(BF16)
    xn_half = (0.5 * xn_f32).astype(BF16)

    def proj(off, width):
        return _dot(xn, w_in_ref[:, off:off + width])

    zb = proj(OFF_B, 2 * HALF)
    cv_buf[CV_HALO:CV_HALO + TILE, :] = zb[:, :HALF] * _sigmoid(zb[:, HALF:])

    cos = cos_ref[...]
    sin = sin_ref[...]
    lane = lax.broadcasted_iota(jnp.int32, (TILE, LANES), 1)
    first_half = (lane % HEAD_DIM) < (HEAD_DIM // 2)
    low_head = lane < HEAD_DIM

    def rotary(t):
        swapped = jnp.where(first_half,
                            pltpu.roll(t, LANES - HEAD_DIM // 2, 1),
                            pltpu.roll(t, HEAD_DIM // 2, 1))
        return t * cos + swapped * sin

    kv = proj(OFF_K, 2 * KV_WIDTH)
    kr = rotary(kv[:, :KV_WIDTH])
    vv = kv[:, KV_WIDTH:]
    for buf, t in ((k_buf, kr), (v_buf, vv)):
        sw = pltpu.roll(t, HEAD_DIM, 1)
        buf[0, WINDOW:WINDOW + TILE, :] = jnp.where(low_head, t, 0.0).astype(BF16)
        buf[1, WINDOW:WINDOW + TILE, :] = jnp.where(low_head, 0.0, sw).astype(BF16)
        buf[2, WINDOW:WINDOW + TILE, :] = jnp.where(low_head, sw, 0.0).astype(BF16)
        buf[3, WINDOW:WINDOW + TILE, :] = jnp.where(low_head, 0.0, t).astype(BF16)

    qz = proj(OFF_Q, Q_WIDTH)
    qi = lax.broadcasted_iota(jnp.int32, (WINDOW, 2 * WINDOW), 0)
    kj = lax.broadcasted_iota(jnp.int32, (WINDOW, 2 * WINDOW), 1)
    band = (kj > qi) & (kj <= qi + WINDOW)
    first_lo = jnp.where(j == 0, WINDOW, 0)
    bias_rest = jnp.where(band, 0.0, MASK_VALUE)
    bias_first = jnp.where(band & (kj >= first_lo), 0.0, MASK_VALUE)
    low_head_w = lax.broadcasted_iota(jnp.int32, (WINDOW, LANES), 1) < HEAD_DIM

    def attention_group(g, after):
        h = (g * 2) // (N_Q_HEADS // N_KV_HEADS)
        qg = rotary(qz[:, g * LANES:(g + 1) * LANES]) * (HEAD_DIM ** -0.5)
        if after is not None:
            qg = qg + after
        qg = qg.astype(BF16)
        for c in range(TILE // WINDOW):
            r0 = c * WINDOW
            bias = bias_first if c == 0 else bias_rest
            qb = qg[r0:r0 + WINDOW, :]
            out = None
            inv = []
            for side in range(2):
                s = _dot_nt(qb, k_buf[2 * h + side, r0:r0 + 2 * WINDOW, :]) + bias
                sink = sink_ref[2 * g + side]
                m = jnp.maximum(jnp.max(s, axis=-1, keepdims=True), sink)
                p = jnp.exp(s - m)
                den = jnp.sum(p, axis=-1, keepdims=True) + jnp.exp(sink - m)
                inv.append(1.0 / den)
                pv = _dot(p.astype(BF16), v_buf[2 * h + side, r0:r0 + 2 * WINDOW, :])
                out = pv if out is None else out + pv
            out = out * jnp.where(low_head_w, inv[0], inv[1])
            yc_s[r0:r0 + WINDOW, g * LANES:(g + 1) * LANES] = out.astype(BF16)

    assert HALF // LANES == N_BRANCH and Q_WIDTH // LANES == N_BRANCH
    conv_cols = []
    gates = []
    after = None
    for n in range(N_BRANCH):
        half_z = _dot(xn_half, w_in_ref[:, OFF_G + n * D_MODEL:OFF_G + (n + 1) * D_MODEL])
        t = jnp.tanh(half_z)
        gates.append(t + 1.0)
        conv_cols.append(jnp.concatenate(
            [_causal_conv_block(cv_buf, cv_w_ref, CV_KERNEL, CV_HALO, r0, n * LANES, after)
             for r0 in range(0, TILE, CONV_ROWS)], axis=0))
        attention_group(n, after)
        after = _zero_after(t)
    for buf in (k_buf, v_buf):
        buf[:, 0:WINDOW, :] = buf[:, TILE:TILE + WINDOW, :]
    conv = jnp.concatenate(conv_cols, axis=1)
    cv_buf[0:CV_HALO, :] = cv_buf[TILE:TILE + CV_HALO, :]

    za = proj(OFF_A, 2 * HALF)
    yb = _layernorm(conv + (cv_b_ref[...] + after[:, 0:1]), cv_g_ref[...], cv_beta_ref[...])
    yb = (yb * _sigmoid(yb)).astype(BF16)

    zd = proj(OFF_D, 3 * HALF)
    ga = 0.5 * za * (1.0 + lax.erf(za * (1.0 / math.sqrt(2.0))))
    u = ga[:, :HALF]
    vln = _layernorm(ga[:, HALF:], sg_g_ref[...], sg_beta_ref[...]).astype(BF16)

    row = lax.broadcasted_iota(jnp.int32, (SG_CHUNK, SG_CHUNK), 0)
    col = lax.broadcasted_iota(jnp.int32, (SG_CHUNK, SG_CHUNK), 1)
    gw = SG_CHUNK
    for g in range(SG_GROUPS):
        w_g = jnp.where(row >= col, sg_w_ref[g], 0.0).astype(BF16)
        b_g = sg_bt_ref[:, g:g + 1]
        for c in range(TILE // SG_CHUNK):
            r0 = c * SG_CHUNK
            mixed = _dot(w_g, vln[r0:r0 + SG_CHUNK, g * gw:(g + 1) * gw]) + b_g
            ya_s[r0:r0 + SG_CHUNK, g * gw:(g + 1) * gw] = (
                u[r0:r0 + SG_CHUNK, g * gw:(g + 1) * gw] * mixed).astype(BF16)

    sc_buf[SC_HALO:SC_HALO + TILE, :] = zd[:, HALF:2 * HALF] * zd[:, 2 * HALF:]
    conv = jnp.concatenate(
        [jnp.concatenate([_causal_conv_block(sc_buf, sc_w_ref, SC_KERNEL, SC_HALO, r0, l0)
                          for r0 in range(0, TILE, CONV_ROWS)], axis=0)
         for l0 in range(0, HALF, LANES)], axis=1)
    sc_buf[0:SC_HALO, :] = sc_buf[TILE:TILE + SC_HALO, :]
    yd = (zd[:, :HALF] * conv).astype(BF16)

    merged = (gates[1] * _dot(yb, w_br_ref[1]) + gates[2] * _dot(yc_s[...], w_br_ref[2])
              + gates[0] * _dot(ya_s[...], w_br_ref[0]) + gates[3] * _dot(yd, w_br_ref[3]))
    o_ref[...] = x + 0.5 * _dot(merged.astype(BF16), w_out_ref[...])


def _ffn_kernel(x_ref, nf_ref, w_gu_ref, w_dn_ref, nfinal_ref, o_ref, *, final_norm):
    x = x_ref[...]
    hn = _rmsnorm(x, nf_ref[...]).astype(BF16)
    gate = _dot(hn, w_gu_ref[:, :D_FF])
    up = _dot(hn, w_gu_ref[:, D_FF:])
    h = (gate * _sigmoid(gate) * up).astype(BF16)
    y = x + _dot(h, w_dn_ref[...])
    if final_norm:
        y = _rmsnorm(y, nfinal_ref[...])
    o_ref[...] = y


def _resident(shape, layer=None):
    if layer is None:
        return pl.BlockSpec(shape, lambda b, j: (0,) * len(shape), pipeline_mode=pl.Buffered(1))
    return pl.BlockSpec((None,) + shape, lambda b, j: (layer,) + (0,) * len(shape),
                        pipeline_mode=pl.Buffered(1))


def _mixer_call(layer, x, cos_t, sin_t, sinks, nm, w_in, sg_g, sg_beta, sg_w, sg_bt,
                cv_w, cv_b, cv_g, cv_beta, sc_w, w_br, w_out):
    batch, seq, _ = x.shape
    tile_spec = pl.BlockSpec((None, TILE, D_MODEL), lambda b, j: (b, j, 0))
    rope_spec = pl.BlockSpec((TILE, LANES), lambda b, j: (j, 0))
    in_specs = [
        pl.BlockSpec(memory_space=pltpu.SMEM),
        tile_spec, rope_spec, rope_spec,
        _resident((1, D_MODEL)),
        _resident((D_MODEL, PROJ_WIDTH), layer),
        _resident((1, HALF)), _resident((1, HALF)),
        _resident((SG_GROUPS, SG_CHUNK, SG_CHUNK)), _resident((SG_CHUNK, SG_GROUPS)),
        _resident((CV_KERNEL, HALF)), _resident((1, HALF)), _resident((1, HALF)), _resident((1, HALF)),
        _resident((SC_KERNEL, HALF)),
        _resident((N_BRANCH, HALF, D_MODEL), layer),
        _resident((D_MODEL, D_MODEL), layer),
    ]
    scratch = [
        pltpu.VMEM((CV_HALO + TILE, HALF), F32),
        pltpu.VMEM((SC_HALO + TILE, HALF), F32),
        pltpu.VMEM((4, WINDOW + TILE, LANES), BF16),
        pltpu.VMEM((4, WINDOW + TILE, LANES), BF16),
        pltpu.VMEM((TILE, HALF), BF16),
        pltpu.VMEM((TILE, HALF), BF16),
    ]
    return pl.pallas_call(
        _mixer_kernel,
        grid=(batch, seq // TILE),
        in_specs=in_specs,
        out_specs=tile_spec,
        out_shape=jax.ShapeDtypeStruct(x.shape, F32),
        scratch_shapes=scratch,
        compiler_params=pltpu.CompilerParams(
            dimension_semantics=("arbitrary", "arbitrary"),
            vmem_limit_bytes=VMEM_LIMIT_BYTES),
        name="mixer",
    )(sinks, x, cos_t, sin_t, nm, w_in, sg_g, sg_beta, sg_w, sg_bt,
      cv_w, cv_b, cv_g, cv_beta, sc_w, w_br, w_out)


def _ffn_call(layer, x, nf, w_gu, w_dn, nfinal, final_norm):
    batch, seq, _ = x.shape
    tile_spec = pl.BlockSpec((None, TILE, D_MODEL), lambda b, j: (b, j, 0))
    return pl.pallas_call(
        functools.partial(_ffn_kernel, final_norm=final_norm),
        grid=(batch, seq // TILE),
        in_specs=[tile_spec, _resident((1, D_MODEL)), _resident((D_MODEL, 2 * D_FF), layer),
                  _resident((D_FF, D_MODEL), layer), _resident((1, D_MODEL))],
        out_specs=tile_spec,
        out_shape=jax.ShapeDtypeStruct(x.shape, F32),
        compiler_params=pltpu.CompilerParams(
            dimension_semantics=("arbitrary", "arbitrary"),
            vmem_limit_bytes=VMEM_LIMIT_BYTES),
        name="ffn",
    )(x, nf, w_gu, w_dn, nfinal)


def _rope_tables(seq):
    pos = jnp.arange(seq, dtype=F32)
    inv_freq = 1.0 / (ROPE_THETA ** (jnp.arange(0, HEAD_DIM, 2, dtype=F32) / HEAD_DIM))
    ang = pos[:, None] * inv_freq[None, :]
    cos = jnp.cos(ang)
    sin = jnp.sin(ang)
    reps = LANES // HEAD_DIM
    cos_t = jnp.tile(jnp.concatenate([cos, cos], axis=-1), (1, reps))
    sin_t = jnp.tile(jnp.concatenate([-sin, sin], axis=-1), (1, reps))
    return cos_t, sin_t


def kernel(x, norm_mix, w_in, sg_ln_g, sg_ln_b, sg_w, sg_b, cv_w, cv_b, cv_ln_g, cv_ln_b,
           attn_sinks, sc_w, w_branch, w_out, norm_ffn, w_gate_up, w_down, norm_final):
    depth = w_in.shape[0]
    seq = x.shape[1]
    assert x.shape[2] == D_MODEL and seq % TILE == 0 and TILE % WINDOW == 0
    cos_t, sin_t = _rope_tables(seq)
    row = lambda a: a.reshape(1, -1)
    w_in, w_branch, w_out, w_gate_up, w_down = (
        w.astype(BF16) for w in (w_in, w_branch, w_out, w_gate_up, w_down))
    for l in range(depth):
        x = _mixer_call(
            l, x, cos_t, sin_t, attn_sinks[l], row(norm_mix[l]), w_in,
            row(sg_ln_g[l]), row(sg_ln_b[l]), sg_w[l], sg_b[l].T,
            cv_w[l], row(cv_b[l]), row(cv_ln_g[l]), row(cv_ln_b[l]), sc_w[l],
            w_branch, w_out)
        x = _ffn_call(l, x, row(norm_ffn[l]), w_gate_up, w_down,
                      row(norm_final), final_norm=(l == depth - 1))
    return x
```

```python
import functools
import math

import jax
import jax.numpy as jnp
from jax import lax
from jax.experimental import pallas as pl
from jax.experimental.pallas import tpu as pltpu

D_MODEL = 1024
HALF = D_MODEL // 2
SG_CHUNK = 128
SG_GROUPS = 4
CV_KERNEL = 31
HEAD_DIM = 64
N_Q_HEADS = 8
N_KV_HEADS = 2
Q_WIDTH = N_Q_HEADS * HEAD_DIM
KV_WIDTH = N_KV_HEADS * HEAD_DIM
WINDOW = 128
ROPE_THETA = 10000.0
SC_KERNEL = 3
N_BRANCH = 4
D_FF = 2816
EPS = 1e-6

OFF_A = 0
OFF_B = OFF_A + 2 * HALF
OFF_Q = OFF_B + 2 * HALF
OFF_K = OFF_Q + Q_WIDTH
OFF_V = OFF_K + KV_WIDTH
OFF_D = OFF_V + KV_WIDTH
OFF_G = OFF_D + 3 * HALF
PROJ_WIDTH = OFF_G + N_BRANCH * D_MODEL

LANES = 128
SUBLANES = 8
TILE = 512
CV_HALO = 32
SC_HALO = SUBLANES
CONV_ROWS = 128
MASK_VALUE = -1e30
VMEM_LIMIT_BYTES = 60 * 1024 * 1024

F32 = jnp.float32
BF16 = jnp.bfloat16


def _sigmoid(x):
    return 0.5 * jnp.tanh(0.5 * x) + 0.5


def _rmsnorm(x, g):
    return x * lax.rsqrt(jnp.mean(x * x, axis=-1, keepdims=True) + EPS) * g


def _layernorm(x, g, b):
    mu = jnp.mean(x, axis=-1, keepdims=True)
    xc = x - mu
    var = jnp.mean(xc * xc, axis=-1, keepdims=True)
    return xc * lax.rsqrt(var + EPS) * g + b


def _dot(a, b):
    return jnp.dot(a, b, preferred_element_type=F32)


def _dot_nt(a, b):
    return lax.dot_general(a, b, (((1,), (1,)), ((), ())), preferred_element_type=F32)


def _zero_after(t):
    return t[-1:, -LANES:] * 0.0


def _causal_conv_block(buf, w_ref, ksize, halo, r0, l0, after=None):
    rows = CONV_ROWS + SUBLANES
    out = None
    for r in range(min(SUBLANES, ksize)):
        z = None
        for a in range((ksize - 1 - r) // SUBLANES + 1):
            tap = ksize - 1 - (SUBLANES * a + r)
            start = halo + r0 - SUBLANES * (a + 1)
            w_tap = w_ref[tap:tap + 1, l0:l0 + LANES]
            if after is not None:
                w_tap = w_tap + after
            term = w_tap * buf[start:start + rows, l0:l0 + LANES]
            z = term if z is None else z + term
        if r:
            z = pltpu.roll(z, r, 0)
        z = z[SUBLANES:, :]
        out = z if out is None else out + z
    return out


def _mixer_kernel(sink_ref, x_ref, cos_ref, sin_ref, nm_ref, w_in_ref,
                  sg_g_ref, sg_beta_ref, sg_w_ref, sg_bt_ref,
                  cv_w_ref, cv_b_ref, cv_g_ref, cv_beta_ref, sc_w_ref,
                  w_br_ref, w_out_ref, o_ref,
                  cv_buf, sc_buf, k_buf, v_buf, ya_s, yc_s):
    j = pl.program_id(1)

    @pl.when(j == 0)
    def _():
        cv_buf[0:CV_HALO, :] = jnp.zeros((CV_HALO, HALF), F32)
        sc_buf[0:SC_HALO, :] = jnp.zeros((SC_HALO, HALF), F32)
        k_buf[:, 0:WINDOW, :] = jnp.zeros((4, WINDOW, LANES), BF16)
        v_buf[:, 0:WINDOW, :] = jnp.zeros((4, WINDOW, LANES), BF16)

    x = x_ref[...]
    xn_f32 = _rmsnorm(x, nm_ref[...])
    xn = xn_f32.astype(BF16)
    xn_half = (0.5 * xn_f32).astype(BF16)

    def proj(off, width):
        return _dot(xn, w_in_ref[:, off:off + width])

    zb = proj(OFF_B, 2 * HALF)
    cv_buf[CV_HALO:CV_HALO + TILE, :] = zb[:, :HALF] * _sigmoid(zb[:, HALF:])

    cos = cos_ref[...]
    sin = sin_ref[...]
    lane = lax.broadcasted_iota(jnp.int32, (TILE, LANES), 1)
    first_half = (lane % HEAD_DIM) < (HEAD_DIM // 2)
    low_head = lane < HEAD_DIM

    def rotary(t):
        swapped = jnp.where(first_half,
                            pltpu.roll(t, LANES - HEAD_DIM // 2, 1),
                            pltpu.roll(t, HEAD_DIM // 2, 1))
        return t * cos + swapped * sin

    kv = proj(OFF_K, 2 * KV_WIDTH)
    kr = rotary(kv[:, :KV_WIDTH])
    vv = kv[:, KV_WIDTH:]
    for buf, t in ((k_buf, kr), (v_buf, vv)):
        sw = pltpu.roll(t, HEAD_DIM, 1)
        buf[0, WINDOW:WINDOW + TILE, :] = jnp.where(low_head, t, 0.0).astype(BF16)
        buf[1, WINDOW:WINDOW + TILE, :] = jnp.where(low_head, 0.0, sw).astype(BF16)
        buf[2, WINDOW:WINDOW + TILE, :] = jnp.where(low_head, sw, 0.0).astype(BF16)
        buf[3, WINDOW:WINDOW + TILE, :] = jnp.where(low_head, 0.0, t).astype(BF16)

    qz = proj(OFF_Q, Q_WIDTH)
    qi = lax.broadcasted_iota(jnp.int32, (WINDOW, 2 * WINDOW), 0)
    kj = lax.broadcasted_iota(jnp.int32, (WINDOW, 2 * WINDOW), 1)
    band = (kj > qi) & (kj <= qi + WINDOW)
    first_lo = jnp.where(j == 0, WINDOW, 0)
    bias_rest = jnp.where(band, 0.0, MASK_VALUE)
    bias_first = jnp.where(band & (kj >= first_lo), 0.0, MASK_VALUE)
    n_chunks = TILE // WINDOW
    bias2_first = jnp.concatenate([bias_first, bias_first], axis=1)
    bias2_rest = jnp.concatenate([bias_rest, bias_rest], axis=1)
    bias_tile = jnp.concatenate([bias2_first] + [bias2_rest] * (n_chunks - 1), axis=0)
    low_head_t = lax.broadcasted_iota(jnp.int32, (TILE, LANES), 1) < HEAD_DIM

    def attention_group(g, after):
        h = (g * 2) // (N_Q_HEADS // N_KV_HEADS)
        qg = rotary(qz[:, g * LANES:(g + 1) * LANES]) * (HEAD_DIM ** -0.5)
        if after is not None:
            qg = qg + after
        qg = qg.astype(BF16)
        windows = [slice(c * WINDOW, c * WINDOW + 2 * WINDOW) for c in range(n_chunks)]
        s = jnp.concatenate(
            [jnp.concatenate([_dot_nt(qg[c * WINDOW:(c + 1) * WINDOW, :], k_buf[2 * h + side, win, :])
                              for side in range(2)], axis=1)
             for c, win in enumerate(windows)], axis=0) + bias_tile
        probs = []
        inv = []
        for side in range(2):
            s_side = s[:, side * 2 * WINDOW:(side + 1) * 2 * WINDOW]
            sink = sink_ref[2 * g + side]
            m = jnp.maximum(jnp.max(s_side, axis=-1, keepdims=True), sink)
            p = jnp.exp(s_side - m)
            den = jnp.sum(p, axis=-1, keepdims=True) + jnp.exp(sink - m)
            inv.append(1.0 / den)
            probs.append(p.astype(BF16))
        out = jnp.concatenate(
            [_dot(probs[0][c * WINDOW:(c + 1) * WINDOW, :], v_buf[2 * h, win, :])
             + _dot(probs[1][c * WINDOW:(c + 1) * WINDOW, :], v_buf[2 * h + 1, win, :])
             for c, win in enumerate(windows)], axis=0)
        out = out * jnp.where(low_head_t, inv[0], inv[1])
        yc_s[:, g * LANES:(g + 1) * LANES] = out.astype(BF16)

    assert HALF // LANES == N_BRANCH and Q_WIDTH // LANES == N_BRANCH
    conv_cols = []
    gates = []
    after = None
    for n in range(N_BRANCH):
        half_z = _dot(xn_half, w_in_ref[:, OFF_G + n * D_MODEL:OFF_G + (n + 1) * D_MODEL])
        t = jnp.tanh(half_z)
        gates.append(t + 1.0)
        conv_cols.append(jnp.concatenate(
            [_causal_conv_block(cv_buf, cv_w_ref, CV_KERNEL, CV_HALO, r0, n * LANES, after)
             for r0 in range(0, TILE, CONV_ROWS)], axis=0))
        attention_group(n, after)
        after = _zero_after(t)
    for buf in (k_buf, v_buf):
        buf[:, 0:WINDOW, :] = buf[:, TILE:TILE + WINDOW, :]
    conv = jnp.concatenate(conv_cols, axis=1)
    cv_buf[0:CV_HALO, :] = cv_buf[TILE:TILE + CV_HALO, :]

    za = proj(OFF_A, 2 * HALF)
    yb = _layernorm(conv + (cv_b_ref[...] + after[:, 0:1]), cv_g_ref[...], cv_beta_ref[...])
    yb = (yb * _sigmoid(yb)).astype(BF16)

    zd = proj(OFF_D, 3 * HALF)
    ga = 0.5 * za * (1.0 + lax.erf(za * (1.0 / math.sqrt(2.0))))
    u = ga[:, :HALF]
    vln = _layernorm(ga[:, HALF:], sg_g_ref[...], sg_beta_ref[...]).astype(BF16)

    row = lax.broadcasted_iota(jnp.int32, (SG_CHUNK, SG_CHUNK), 0)
    col = lax.broadcasted_iota(jnp.int32, (SG_CHUNK, SG_CHUNK), 1)
    gw = SG_CHUNK
    for g in range(SG_GROUPS):
        w_g = jnp.where(row >= col, sg_w_ref[g], 0.0).astype(BF16)
        b_g = sg_bt_ref[:, g:g + 1]
        mixed = jnp.concatenate(
            [_dot(w_g, vln[r0:r0 + SG_CHUNK, g * gw:(g + 1) * gw]) + b_g
             for r0 in range(0, TILE, SG_CHUNK)], axis=0)
        ya_s[:, g * gw:(g + 1) * gw] = (u[:, g * gw:(g + 1) * gw] * mixed).astype(BF16)

    sc_buf[SC_HALO:SC_HALO + TILE, :] = zd[:, HALF:2 * HALF] * zd[:, 2 * HALF:]
    conv = jnp.concatenate(
        [jnp.concatenate([_causal_conv_block(sc_buf, sc_w_ref, SC_KERNEL, SC_HALO, r0, l0)
                          for r0 in range(0, TILE, CONV_ROWS)], axis=0)
         for l0 in range(0, HALF, LANES)], axis=1)
    sc_buf[0:SC_HALO, :] = sc_buf[TILE:TILE + SC_HALO, :]
    yd = (zd[:, :HALF] * conv).astype(BF16)

    merged = (gates[1] * _dot(yb, w_br_ref[1]) + gates[2] * _dot(yc_s[...], w_br_ref[2])
              + gates[0] * _dot(ya_s[...], w_br_ref[0]) + gates[3] * _dot(yd, w_br_ref[3]))
    o_ref[...] = x + 0.5 * _dot(merged.astype(BF16), w_out_ref[...])


def _ffn_kernel(x_ref, nf_ref, w_gu_ref, w_dn_ref, nfinal_ref, o_ref, *, final_norm):
    x = x_ref[...]
    hn = _rmsnorm(x, nf_ref[...]).astype(BF16)
    gate = _dot(hn, w_gu_ref[:, :D_FF])
    up = _dot(hn, w_gu_ref[:, D_FF:])
    h = (gate * _sigmoid(gate) * up).astype(BF16)
    y = x + _dot(h, w_dn_ref[...])
    if final_norm:
        y = _rmsnorm(y, nfinal_ref[...])
    o_ref[...] = y


def _resident(shape, layer=None):
    if layer is None:
        return pl.BlockSpec(shape, lambda b, j: (0,) * len(shape), pipeline_mode=pl.Buffered(1))
    return pl.BlockSpec((None,) + shape, lambda b, j: (layer,) + (0,) * len(shape),
                        pipeline_mode=pl.Buffered(1))


def _mixer_call(layer, x, cos_t, sin_t, sinks, nm, w_in, sg_g, sg_beta, sg_w, sg_bt,
                cv_w, cv_b, cv_g, cv_beta, sc_w, w_br, w_out):
    batch, seq, _ = x.shape
    tile_spec = pl.BlockSpec((None, TILE, D_MODEL), lambda b, j: (b, j, 0))
    rope_spec = pl.BlockSpec((TILE, LANES), lambda b, j: (j, 0))
    in_specs = [
        pl.BlockSpec(memory_space=pltpu.SMEM),
        tile_spec, rope_spec, rope_spec,
        _resident((1, D_MODEL)),
        _resident((D_MODEL, PROJ_WIDTH), layer),
        _resident((1, HALF)), _resident((1, HALF)),
        _resident((SG_GROUPS, SG_CHUNK, SG_CHUNK)), _resident((SG_CHUNK, SG_GROUPS)),
        _resident((CV_KERNEL, HALF)), _resident((1, HALF)), _resident((1, HALF)), _resident((1, HALF)),
        _resident((SC_KERNEL, HALF)),
        _resident((N_BRANCH, HALF, D_MODEL), layer),
        _resident((D_MODEL, D_MODEL), layer),
    ]
    scratch = [
        pltpu.VMEM((CV_HALO + TILE, HALF), F32),
        pltpu.VMEM((SC_HALO + TILE, HALF), F32),
        pltpu.VMEM((4, WINDOW + TILE, LANES), BF16),
        pltpu.VMEM((4, WINDOW + TILE, LANES), BF16),
        pltpu.VMEM((TILE, HALF), BF16),
        pltpu.VMEM((TILE, HALF), BF16),
    ]
    return pl.pallas_call(
        _mixer_kernel,
        grid=(batch, seq // TILE),
        in_specs=in_specs,
        out_specs=tile_spec,
        out_shape=jax.ShapeDtypeStruct(x.shape, F32),
        scratch_shapes=scratch,
        compiler_params=pltpu.CompilerParams(
            dimension_semantics=("arbitrary", "arbitrary"),
            vmem_limit_bytes=VMEM_LIMIT_BYTES),
        name="mixer",
    )(sinks, x, cos_t, sin_t, nm, w_in, sg_g, sg_beta, sg_w, sg_bt,
      cv_w, cv_b, cv_g, cv_beta, sc_w, w_br, w_out)


def _ffn_call(layer, x, nf, w_gu, w_dn, nfinal, final_norm):
    batch, seq, _ = x.shape
    tile_spec = pl.BlockSpec((None, TILE, D_MODEL), lambda b, j: (b, j, 0))
    return pl.pallas_call(
        functools.partial(_ffn_kernel, final_norm=final_norm),
        grid=(batch, seq // TILE),
        in_specs=[tile_spec, _resident((1, D_MODEL)), _resident((D_MODEL, 2 * D_FF), layer),
                  _resident((D_FF, D_MODEL), layer), _resident((1, D_MODEL))],
        out_specs=tile_spec,
        out_shape=jax.ShapeDtypeStruct(x.shape, F32),
        compiler_params=pltpu.CompilerParams(
            dimension_semantics=("arbitrary", "arbitrary"),
            vmem_limit_bytes=VMEM_LIMIT_BYTES),
        name="ffn",
    )(x, nf, w_gu, w_dn, nfinal)


def _rope_tables(seq):
    pos = jnp.arange(seq, dtype=F32)
    inv_freq = 1.0 / (ROPE_THETA ** (jnp.arange(0, HEAD_DIM, 2, dtype=F32) / HEAD_DIM))
    ang = pos[:, None] * inv_freq[None, :]
    cos = jnp.cos(ang)
    sin = jnp.sin(ang)
    reps = LANES // HEAD_DIM
    cos_t = jnp.tile(jnp.concatenate([cos, cos], axis=-1), (1, reps))
    sin_t = jnp.tile(jnp.concatenate([-sin, sin], axis=-1), (1, reps))
    return cos_t, sin_t


def kernel(x, norm_mix, w_in, sg_ln_g, sg_ln_b, sg_w, sg_b, cv_w, cv_b, cv_ln_g, cv_ln_b,
           attn_sinks, sc_w, w_branch, w_out, norm_ffn, w_gate_up, w_down, norm_final):
    depth = w_in.shape[0]
    seq = x.shape[1]
    assert x.shape[2] == D_MODEL and seq % TILE == 0 and TILE % WINDOW == 0
    cos_t, sin_t = _rope_tables(seq)
    row = lambda a: a.reshape(1, -1)
    w_in, w_branch, w_out, w_gate_up, w_down = (
        w.astype(BF16) for w in (w_in, w_branch, w_out, w_gate_up, w_down))
    for l in range(depth):
        x = _mixer_call(
            l, x, cos_t, sin_t, attn_sinks[l], row(norm_mix[l]), w_in,
            row(sg_ln_g[l]), row(sg_ln_b[l]), sg_w[l], sg_b[l].T,
            cv_w[l], row(cv_b[l]), row(cv_ln_g[l]), row(cv_ln_b[l]), sc_w[l],
            w_branch, w_out)
        x = _ffn_call(l, x, row(norm_ffn[l]), w_gate_up, w_down,
                      row(norm_final), final_norm=(l == depth - 1))
    return x
```

```python
import functools
import math

import jax
import jax.numpy as jnp
from jax import lax
from jax.experimental import pallas as pl
from jax.experimental.pallas import tpu as pltpu

D_MODEL = 1024
HALF = D_MODEL // 2
SG_CHUNK = 128
SG_GROUPS = 4
CV_KERNEL = 31
HEAD_DIM = 64
N_Q_HEADS = 8
N_KV_HEADS = 2
Q_WIDTH = N_Q_HEADS * HEAD_DIM
KV_WIDTH = N_KV_HEADS * HEAD_DIM
WINDOW = 128
ROPE_THETA = 10000.0
SC_KERNEL = 3
N_BRANCH = 4
D_FF = 2816
EPS = 1e-6

OFF_A = 0
OFF_B = OFF_A + 2 * HALF
OFF_Q = OFF_B + 2 * HALF
OFF_K = OFF_Q + Q_WIDTH
OFF_V = OFF_K + KV_WIDTH
OFF_D = OFF_V + KV_WIDTH
OFF_G = OFF_D + 3 * HALF
PROJ_WIDTH = OFF_G + N_BRANCH * D_MODEL

LANES = 128
SUBLANES = 8
TILE = 512
CV_HALO = 32
SC_HALO = SUBLANES
CONV_ROWS = 128
STAGE_ROWS_WIDE = 16
STAGE_ROWS_NARROW = 128
MASK_VALUE = -1e30
VMEM_LIMIT_BYTES = 60 * 1024 * 1024

F32 = jnp.float32
BF16 = jnp.bfloat16


def _sigmoid(x):
    return 0.5 * jnp.tanh(0.5 * x) + 0.5


def _rmsnorm(x, g):
    return x * lax.rsqrt(jnp.mean(x * x, axis=-1, keepdims=True) + EPS) * g


def _layernorm(x, g, b):
    mu = jnp.mean(x, axis=-1, keepdims=True)
    xc = x - mu
    var = jnp.mean(xc * xc, axis=-1, keepdims=True)
    return xc * lax.rsqrt(var + EPS) * g + b


def _dot(a, b):
    return jnp.dot(a, b, preferred_element_type=F32)


def _dot_nt(a, b):
    return lax.dot_general(a, b, (((1,), (1,)), ((), ())), preferred_element_type=F32)


def _load_cast(src_hbm, dst, stage, sem):
    rows = stage.shape[1]
    n_chunks = src_hbm.shape[0] // rows
    assert n_chunks * rows == src_hbm.shape[0] and stage.shape[2] == src_hbm.shape[1]

    def copy(k, slot):
        return pltpu.make_async_copy(
            src_hbm.at[pl.ds(k * rows, rows), :], stage.at[slot], sem.at[slot])

    copy(0, 0).start()

    def body(k, carry):
        slot = lax.rem(k, 2)

        @pl.when(k + 1 < n_chunks)
        def _():
            copy(k + 1, 1 - slot).start()

        copy(k, slot).wait()
        dst[pl.ds(pl.multiple_of(k * rows, rows), rows), :] = stage[slot].astype(BF16)
        return carry

    lax.fori_loop(0, n_chunks, body, 0)


def _zero_after(t):
    return t[-1:, -LANES:] * 0.0


def _causal_conv_block(buf, w_ref, ksize, halo, r0, l0, after=None):
    rows = CONV_ROWS + SUBLANES
    out = None
    for r in range(min(SUBLANES, ksize)):
        z = None
        for a in range((ksize - 1 - r) // SUBLANES + 1):
            tap = ksize - 1 - (SUBLANES * a + r)
            start = halo + r0 - SUBLANES * (a + 1)
            w_tap = w_ref[tap:tap + 1, l0:l0 + LANES]
            if after is not None:
                w_tap = w_tap + after
            term = w_tap * buf[start:start + rows, l0:l0 + LANES]
            z = term if z is None else z + term
        if r:
            z = pltpu.roll(z, r, 0)
        z = z[SUBLANES:, :]
        out = z if out is None else out + z
    return out


def _mixer_kernel(sink_ref, x_ref, cos_ref, sin_ref, nm_ref, w_in_hbm,
                  sg_g_ref, sg_beta_ref, sg_w_ref, sg_bt_ref,
                  cv_w_ref, cv_b_ref, cv_g_ref, cv_beta_ref, sc_w_ref,
                  w_br_hbm, w_out_hbm, o_ref,
                  cv_buf, sc_buf, k_buf, v_buf, ya_s, yc_s,
                  w_in_ref, w_br_ref, w_out_ref, stage_wide, stage_narrow, dma_sem, *, layer):
    j = pl.program_id(1)

    @pl.when((pl.program_id(0) == 0) & (j == 0))
    def _():
        _load_cast(w_in_hbm.at[layer], w_in_ref, stage_wide, dma_sem)
        for n in range(N_BRANCH):
            _load_cast(w_br_hbm.at[layer, n], w_br_ref.at[n], stage_narrow, dma_sem)
        _load_cast(w_out_hbm.at[layer], w_out_ref, stage_narrow, dma_sem)

    @pl.when(j == 0)
    def _():
        cv_buf[0:CV_HALO, :] = jnp.zeros((CV_HALO, HALF), F32)
        sc_buf[0:SC_HALO, :] = jnp.zeros((SC_HALO, HALF), F32)
        k_buf[:, 0:WINDOW, :] = jnp.zeros((4, WINDOW, LANES), BF16)
        v_buf[:, 0:WINDOW, :] = jnp.zeros((4, WINDOW, LANES), BF16)

    x = x_ref[...]
    xn_f32 = _rmsnorm(x, nm_ref[...])
    xn = xn_f32.astype(BF16)
    xn_half = (0.5 * xn_f32).astype(BF16)

    def proj(off, width):
        return _dot(xn, w_in_ref[:, off:off + width])

    zb = proj(OFF_B, 2 * HALF)
    cv_buf[CV_HALO:CV_HALO + TILE, :] = zb[:, :HALF] * _sigmoid(zb[:, HALF:])

    cos = cos_ref[...]
    sin = sin_ref[...]
    lane = lax.broadcasted_iota(jnp.int32, (TILE, LANES), 1)
    first_half = (lane % HEAD_DIM) < (HEAD_DIM // 2)
    low_head = lane < HEAD_DIM

    def rotary(t):
        swapped = jnp.where(first_half,
                            pltpu.roll(t, LANES - HEAD_DIM // 2, 1),
                            pltpu.roll(t, HEAD_DIM // 2, 1))
        return t * cos + swapped * sin

    kv = proj(OFF_K, 2 * KV_WIDTH)
    kr = rotary(kv[:, :KV_WIDTH])
    vv = kv[:, KV_WIDTH:]
    for buf, t in ((k_buf, kr), (v_buf, vv)):
        sw = pltpu.roll(t, HEAD_DIM, 1)
        buf[0, WINDOW:WINDOW + TILE, :] = jnp.where(low_head, t, 0.0).astype(BF16)
        buf[1, WINDOW:WINDOW + TILE, :] = jnp.where(low_head, 0.0, sw).astype(BF16)
        buf[2, WINDOW:WINDOW + TILE, :] = jnp.where(low_head, sw, 0.0).astype(BF16)
        buf[3, WINDOW:WINDOW + TILE, :] = jnp.where(low_head, 0.0, t).astype(BF16)

    qz = proj(OFF_Q, Q_WIDTH)
    qi = lax.broadcasted_iota(jnp.int32, (WINDOW, 2 * WINDOW), 0)
    kj = lax.broadcasted_iota(jnp.int32, (WINDOW, 2 * WINDOW), 1)
    band = (kj > qi) & (kj <= qi + WINDOW)
    first_lo = jnp.where(j == 0, WINDOW, 0)
    bias_rest = jnp.where(band, 0.0, MASK_VALUE)
    bias_first = jnp.where(band & (kj >= first_lo), 0.0, MASK_VALUE)
    n_chunks = TILE // WINDOW
    bias2_first = jnp.concatenate([bias_first, bias_first], axis=1)
    bias2_rest = jnp.concatenate([bias_rest, bias_rest], axis=1)
    bias_tile = jnp.concatenate([bias2_first] + [bias2_rest] * (n_chunks - 1), axis=0)
    low_head_t = lax.broadcasted_iota(jnp.int32, (TILE, LANES), 1) < HEAD_DIM

    def attention_group(g, after):
        h = (g * 2) // (N_Q_HEADS // N_KV_HEADS)
        qg = rotary(qz[:, g * LANES:(g + 1) * LANES]) * (HEAD_DIM ** -0.5)
        if after is not None:
            qg = qg + after
        qg = qg.astype(BF16)
        windows = [slice(c * WINDOW, c * WINDOW + 2 * WINDOW) for c in range(n_chunks)]
        s = jnp.concatenate(
            [jnp.concatenate([_dot_nt(qg[c * WINDOW:(c + 1) * WINDOW, :], k_buf[2 * h + side, win, :])
                              for side in range(2)], axis=1)
             for c, win in enumerate(windows)], axis=0) + bias_tile
        probs = []
        inv = []
        for side in range(2):
            s_side = s[:, side * 2 * WINDOW:(side + 1) * 2 * WINDOW]
            sink = sink_ref[2 * g + side]
            m = jnp.maximum(jnp.max(s_side, axis=-1, keepdims=True), sink)
            p = jnp.exp(s_side - m)
            den = jnp.sum(p, axis=-1, keepdims=True) + jnp.exp(sink - m)
            inv.append(1.0 / den)
            probs.append(p.astype(BF16))
        out = jnp.concatenate(
            [_dot(probs[0][c * WINDOW:(c + 1) * WINDOW, :], v_buf[2 * h, win, :])
             + _dot(probs[1][c * WINDOW:(c + 1) * WINDOW, :], v_buf[2 * h + 1, win, :])
             for c, win in enumerate(windows)], axis=0)
        out = out * jnp.where(low_head_t, inv[0], inv[1])
        yc_s[:, g * LANES:(g + 1) * LANES] = out.astype(BF16)

    assert HALF // LANES == N_BRANCH and Q_WIDTH // LANES == N_BRANCH
    conv_cols = []
    gates = []
    after = None
    for n in range(N_BRANCH):
        half_z = _dot(xn_half, w_in_ref[:, OFF_G + n * D_MODEL:OFF_G + (n + 1) * D_MODEL])
        t = jnp.tanh(half_z)
        gates.append(t + 1.0)
        conv_cols.append(jnp.concatenate(
            [_causal_conv_block(cv_buf, cv_w_ref, CV_KERNEL, CV_HALO, r0, n * LANES, after)
             for r0 in range(0, TILE, CONV_ROWS)], axis=0))
        attention_group(n, after)
        after = _zero_after(t)
    for buf in (k_buf, v_buf):
        buf[:, 0:WINDOW, :] = buf[:, TILE:TILE + WINDOW, :]
    conv = jnp.concatenate(conv_cols, axis=1)
    cv_buf[0:CV_HALO, :] = cv_buf[TILE:TILE + CV_HALO, :]

    za = proj(OFF_A, 2 * HALF)
    yb = _layernorm(conv + (cv_b_ref[...] + after[:, 0:1]), cv_g_ref[...], cv_beta_ref[...])
    yb = (yb * _sigmoid(yb)).astype(BF16)

    zd = proj(OFF_D, 3 * HALF)
    ga = 0.5 * za * (1.0 + lax.erf(za * (1.0 / math.sqrt(2.0))))
    u = ga[:, :HALF]
    vln = _layernorm(ga[:, HALF:], sg_g_ref[...], sg_beta_ref[...]).astype(BF16)

    row = lax.broadcasted_iota(jnp.int32, (SG_CHUNK, SG_CHUNK), 0)
    col = lax.broadcasted_iota(jnp.int32, (SG_CHUNK, SG_CHUNK), 1)
    gw = SG_CHUNK
    for g in range(SG_GROUPS):
        w_g = jnp.where(row >= col, sg_w_ref[g], 0.0).astype(BF16)
        b_g = sg_bt_ref[:, g:g + 1]
        mixed = jnp.concatenate(
            [_dot(w_g, vln[r0:r0 + SG_CHUNK, g * gw:(g + 1) * gw]) + b_g
             for r0 in range(0, TILE, SG_CHUNK)], axis=0)
        ya_s[:, g * gw:(g + 1) * gw] = (u[:, g * gw:(g + 1) * gw] * mixed).astype(BF16)

    sc_buf[SC_HALO:SC_HALO + TILE, :] = zd[:, HALF:2 * HALF] * zd[:, 2 * HALF:]
    conv = jnp.concatenate(
        [jnp.concatenate([_causal_conv_block(sc_buf, sc_w_ref, SC_KERNEL, SC_HALO, r0, l0)
                          for r0 in range(0, TILE, CONV_ROWS)], axis=0)
         for l0 in range(0, HALF, LANES)], axis=1)
    sc_buf[0:SC_HALO, :] = sc_buf[TILE:TILE + SC_HALO, :]
    yd = (zd[:, :HALF] * conv).astype(BF16)

    merged = (gates[1] * _dot(yb, w_br_ref[1]) + gates[2] * _dot(yc_s[...], w_br_ref[2])
              + gates[0] * _dot(ya_s[...], w_br_ref[0]) + gates[3] * _dot(yd, w_br_ref[3]))
    o_ref[...] = x + 0.5 * _dot(merged.astype(BF16), w_out_ref[...])


def _ffn_kernel(x_ref, nf_ref, w_gu_hbm, w_dn_hbm, nfinal_ref, o_ref,
                w_gu_ref, w_dn_ref, stage_wide, stage_narrow, dma_sem, *, layer, final_norm):
    @pl.when((pl.program_id(0) == 0) & (pl.program_id(1) == 0))
    def _():
        _load_cast(w_gu_hbm.at[layer], w_gu_ref, stage_wide, dma_sem)
        _load_cast(w_dn_hbm.at[layer], w_dn_ref, stage_narrow, dma_sem)

    x = x_ref[...]
    hn = _rmsnorm(x, nf_ref[...]).astype(BF16)
    gate = _dot(hn, w_gu_ref[:, :D_FF])
    up = _dot(hn, w_gu_ref[:, D_FF:])
    h = (gate * _sigmoid(gate) * up).astype(BF16)
    y = x + _dot(h, w_dn_ref[...])
    if final_norm:
        y = _rmsnorm(y, nfinal_ref[...])
    o_ref[...] = y


def _resident(shape):
    return pl.BlockSpec(shape, lambda b, j: (0,) * len(shape), pipeline_mode=pl.Buffered(1))


def _mixer_call(layer, x, cos_t, sin_t, sinks, nm, w_in, sg_g, sg_beta, sg_w, sg_bt,
                cv_w, cv_b, cv_g, cv_beta, sc_w, w_br, w_out):
    batch, seq, _ = x.shape
    tile_spec = pl.BlockSpec((None, TILE, D_MODEL), lambda b, j: (b, j, 0))
    rope_spec = pl.BlockSpec((TILE, LANES), lambda b, j: (j, 0))
    in_specs = [
        pl.BlockSpec(memory_space=pltpu.SMEM),
        tile_spec, rope_spec, rope_spec,
        _resident((1, D_MODEL)),
        pl.BlockSpec(memory_space=pl.ANY),
        _resident((1, HALF)), _resident((1, HALF)),
        _resident((SG_GROUPS, SG_CHUNK, SG_CHUNK)), _resident((SG_CHUNK, SG_GROUPS)),
        _resident((CV_KERNEL, HALF)), _resident((1, HALF)), _resident((1, HALF)), _resident((1, HALF)),
        _resident((SC_KERNEL, HALF)),
        pl.BlockSpec(memory_space=pl.ANY),
        pl.BlockSpec(memory_space=pl.ANY),
    ]
    scratch = [
        pltpu.VMEM((CV_HALO + TILE, HALF), F32),
        pltpu.VMEM((SC_HALO + TILE, HALF), F32),
        pltpu.VMEM((4, WINDOW + TILE, LANES), BF16),
        pltpu.VMEM((4, WINDOW + TILE, LANES), BF16),
        pltpu.VMEM((TILE, HALF), BF16),
        pltpu.VMEM((TILE, HALF), BF16),
        pltpu.VMEM((D_MODEL, PROJ_WIDTH), BF16),
        pltpu.VMEM((N_BRANCH, HALF, D_MODEL), BF16),
        pltpu.VMEM((D_MODEL, D_MODEL), BF16),
        pltpu.VMEM((2, STAGE_ROWS_WIDE, PROJ_WIDTH), F32),
        pltpu.VMEM((2, STAGE_ROWS_NARROW, D_MODEL), F32),
        pltpu.SemaphoreType.DMA((2,)),
    ]
    return pl.pallas_call(
        functools.partial(_mixer_kernel, layer=layer),
        grid=(batch, seq // TILE),
        in_specs=in_specs,
        out_specs=tile_spec,
        out_shape=jax.ShapeDtypeStruct(x.shape, F32),
        scratch_shapes=scratch,
        compiler_params=pltpu.CompilerParams(
            dimension_semantics=("arbitrary", "arbitrary"),
            vmem_limit_bytes=VMEM_LIMIT_BYTES),
        name="mixer",
    )(sinks, x, cos_t, sin_t, nm, w_in, sg_g, sg_beta, sg_w, sg_bt,
      cv_w, cv_b, cv_g, cv_beta, sc_w, w_br, w_out)


def _ffn_call(layer, x, nf, w_gu, w_dn, nfinal, final_norm):
    batch, seq, _ = x.shape
    tile_spec = pl.BlockSpec((None, TILE, D_MODEL), lambda b, j: (b, j, 0))
    hbm = pl.BlockSpec(memory_space=pl.ANY)
    return pl.pallas_call(
        functools.partial(_ffn_kernel, layer=layer, final_norm=final_norm),
        grid=(batch, seq // TILE),
        in_specs=[tile_spec, _resident((1, D_MODEL)), hbm, hbm, _resident((1, D_MODEL))],
        out_specs=tile_spec,
        out_shape=jax.ShapeDtypeStruct(x.shape, F32),
        scratch_shapes=[
            pltpu.VMEM((D_MODEL, 2 * D_FF), BF16),
            pltpu.VMEM((D_FF, D_MODEL), BF16),
            pltpu.VMEM((2, STAGE_ROWS_WIDE, 2 * D_FF), F32),
            pltpu.VMEM((2, STAGE_ROWS_NARROW, D_MODEL), F32),
            pltpu.SemaphoreType.DMA((2,)),
        ],
        compiler_params=pltpu.CompilerParams(
            dimension_semantics=("arbitrary", "arbitrary"),
            vmem_limit_bytes=VMEM_LIMIT_BYTES),
        name="ffn",
    )(x, nf, w_gu, w_dn, nfinal)


def _rope_tables(seq):
    pos = jnp.arange(seq, dtype=F32)
    inv_freq = 1.0 / (ROPE_THETA ** (jnp.arange(0, HEAD_DIM, 2, dtype=F32) / HEAD_DIM))
    ang = pos[:, None] * inv_freq[None, :]
    cos = jnp.cos(ang)
    sin = jnp.sin(ang)
    reps = LANES // HEAD_DIM
    cos_t = jnp.tile(jnp.concatenate([cos, cos], axis=-1), (1, reps))
    sin_t = jnp.tile(jnp.concatenate([-sin, sin], axis=-1), (1, reps))
    return cos_t, sin_t


def kernel(x, norm_mix, w_in, sg_ln_g, sg_ln_b, sg_w, sg_b, cv_w, cv_b, cv_ln_g, cv_ln_b,
           attn_sinks, sc_w, w_branch, w_out, norm_ffn, w_gate_up, w_down, norm_final):
    depth = w_in.shape[0]
    seq = x.shape[1]
    assert x.shape[2] == D_MODEL and seq % TILE == 0 and TILE % WINDOW == 0
    cos_t, sin_t = _rope_tables(seq)
    row = lambda a: a.reshape(1, -1)
    for l in range(depth):
        x = _mixer_call(
            l, x, cos_t, sin_t, attn_sinks[l], row(norm_mix[l]), w_in,
            row(sg_ln_g[l]), row(sg_ln_b[l]), sg_w[l], sg_b[l].T,
            cv_w[l], row(cv_b[l]), row(cv_ln_g[l]), row(cv_ln_b[l]), sc_w[l],
            w_branch, w_out)
        x = _ffn_call(l, x, row(norm_ffn[l]), w_gate_up, w_down,
                      row(norm_final), final_norm=(l == depth - 1))
    return x
```

```python
import functools
import math

import jax
import jax.numpy as jnp
from jax import lax
from jax.experimental import pallas as pl
from jax.experimental.pallas import tpu as pltpu

D_MODEL = 1024
HALF = D_MODEL // 2
SG_CHUNK = 128
SG_GROUPS = 4
CV_KERNEL = 31
HEAD_DIM = 64
N_Q_HEADS = 8
N_KV_HEADS = 2
Q_WIDTH = N_Q_HEADS * HEAD_DIM
KV_WIDTH = N_KV_HEADS * HEAD_DIM
WINDOW = 128
ROPE_THETA = 10000.0
SC_KERNEL = 3
N_BRANCH = 4
D_FF = 2816
EPS = 1e-6

OFF_A = 0
OFF_B = OFF_A + 2 * HALF
OFF_Q = OFF_B + 2 * HALF
OFF_K = OFF_Q + Q_WIDTH
OFF_V = OFF_K + KV_WIDTH
OFF_D = OFF_V + KV_WIDTH
OFF_G = OFF_D + 3 * HALF
PROJ_WIDTH = OFF_G + N_BRANCH * D_MODEL

LANES = 128
SUBLANES = 8
TILE = 512
CV_HALO = 32
SC_HALO = SUBLANES
CONV_ROWS = 128
STAGE_SLOTS = 4
STAGE_ROWS_WIDE = 16
STAGE_ROWS_NARROW = 64
MASK_VALUE = -1e30
VMEM_LIMIT_BYTES = 60 * 1024 * 1024

F32 = jnp.float32
BF16 = jnp.bfloat16


def _sigmoid(x):
    return 0.5 * jnp.tanh(0.5 * x) + 0.5


def _rmsnorm(x, g):
    return x * lax.rsqrt(jnp.mean(x * x, axis=-1, keepdims=True) + EPS) * g


def _layernorm(x, g, b):
    mu = jnp.mean(x, axis=-1, keepdims=True)
    xc = x - mu
    var = jnp.mean(xc * xc, axis=-1, keepdims=True)
    return xc * lax.rsqrt(var + EPS) * g + b


def _dot(a, b):
    return jnp.dot(a, b, preferred_element_type=F32)


def _dot_nt(a, b):
    return lax.dot_general(a, b, (((1,), (1,)), ((), ())), preferred_element_type=F32)


def _load_cast(src_hbm, dst, stage, sem):
    n_slots, rows, _ = stage.shape
    n_chunks = src_hbm.shape[0] // rows
    assert n_chunks * rows == src_hbm.shape[0] and stage.shape[2] == src_hbm.shape[1]
    assert n_chunks >= n_slots - 1

    def copy(k, slot):
        return pltpu.make_async_copy(
            src_hbm.at[pl.ds(k * rows, rows), :], stage.at[slot], sem.at[slot])

    for k in range(n_slots - 1):
        copy(k, k).start()

    def body(k, carry):
        slot = lax.rem(k, n_slots)
        ahead = k + (n_slots - 1)

        @pl.when(ahead < n_chunks)
        def _():
            copy(ahead, lax.rem(ahead, n_slots)).start()

        copy(k, slot).wait()
        dst[pl.ds(pl.multiple_of(k * rows, rows), rows), :] = stage[slot].astype(BF16)
        return carry

    lax.fori_loop(0, n_chunks, body, 0)


def _zero_after(t):
    return t[-1:, -LANES:] * 0.0


def _causal_conv_block(buf, w_ref, ksize, halo, r0, l0, after=None):
    rows = CONV_ROWS + SUBLANES
    out = None
    for r in range(min(SUBLANES, ksize)):
        z = None
        for a in range((ksize - 1 - r) // SUBLANES + 1):
            tap = ksize - 1 - (SUBLANES * a + r)
            start = halo + r0 - SUBLANES * (a + 1)
            w_tap = w_ref[tap:tap + 1, l0:l0 + LANES]
            if after is not None:
                w_tap = w_tap + after
            term = w_tap * buf[start:start + rows, l0:l0 + LANES]
            z = term if z is None else z + term
        if r:
            z = pltpu.roll(z, r, 0)
        z = z[SUBLANES:, :]
        out = z if out is None else out + z
    return out


def _mixer_kernel(sink_ref, x_ref, cos_ref, sin_ref, nm_ref, w_in_hbm,
                  sg_g_ref, sg_beta_ref, sg_w_ref, sg_bt_ref,
                  cv_w_ref, cv_b_ref, cv_g_ref, cv_beta_ref, sc_w_ref,
                  w_br_hbm, w_out_hbm, o_ref,
                  cv_buf, sc_buf, k_buf, v_buf, ya_s, yc_s,
                  w_in_ref, w_br_ref, w_out_ref, stage_wide, stage_narrow, dma_sem, *, layer):
    j = pl.program_id(1)

    @pl.when((pl.program_id(0) == 0) & (j == 0))
    def _():
        _load_cast(w_in_hbm.at[layer], w_in_ref, stage_wide, dma_sem)
        for n in range(N_BRANCH):
            _load_cast(w_br_hbm.at[layer, n], w_br_ref.at[n], stage_narrow, dma_sem)
        _load_cast(w_out_hbm.at[layer], w_out_ref, stage_narrow, dma_sem)

    @pl.when(j == 0)
    def _():
        cv_buf[0:CV_HALO, :] = jnp.zeros((CV_HALO, HALF), F32)
        sc_buf[0:SC_HALO, :] = jnp.zeros((SC_HALO, HALF), F32)
        k_buf[:, 0:WINDOW, :] = jnp.zeros((4, WINDOW, LANES), BF16)
        v_buf[:, 0:WINDOW, :] = jnp.zeros((4, WINDOW, LANES), BF16)

    x = x_ref[...]
    xn_f32 = _rmsnorm(x, nm_ref[...])
    xn = xn_f32.astype(BF16)
    xn_half = (0.5 * xn_f32).astype(BF16)

    def proj(off, width):
        return _dot(xn, w_in_ref[:, off:off + width])

    zb = proj(OFF_B, 2 * HALF)
    cv_buf[CV_HALO:CV_HALO + TILE, :] = zb[:, :HALF] * _sigmoid(zb[:, HALF:])

    cos = cos_ref[...]
    sin = sin_ref[...]
    lane = lax.broadcasted_iota(jnp.int32, (TILE, LANES), 1)
    first_half = (lane % HEAD_DIM) < (HEAD_DIM // 2)
    low_head = lane < HEAD_DIM

    def rotary(t):
        swapped = jnp.where(first_half,
                            pltpu.roll(t, LANES - HEAD_DIM // 2, 1),
                            pltpu.roll(t, HEAD_DIM // 2, 1))
        return t * cos + swapped * sin

    kv = proj(OFF_K, 2 * KV_WIDTH)
    kr = rotary(kv[:, :KV_WIDTH])
    vv = kv[:, KV_WIDTH:]
    for buf, t in ((k_buf, kr), (v_buf, vv)):
        sw = pltpu.roll(t, HEAD_DIM, 1)
        buf[0, WINDOW:WINDOW + TILE, :] = jnp.where(low_head, t, 0.0).astype(BF16)
        buf[1, WINDOW:WINDOW + TILE, :] = jnp.where(low_head, 0.0, sw).astype(BF16)
        buf[2, WINDOW:WINDOW + TILE, :] = jnp.where(low_head, sw, 0.0).astype(BF16)
        buf[3, WINDOW:WINDOW + TILE, :] = jnp.where(low_head, 0.0, t).astype(BF16)

    qz = proj(OFF_Q, Q_WIDTH)
    qi = lax.broadcasted_iota(jnp.int32, (WINDOW, 2 * WINDOW), 0)
    kj = lax.broadcasted_iota(jnp.int32, (WINDOW, 2 * WINDOW), 1)
    band = (kj > qi) & (kj <= qi + WINDOW)
    first_lo = jnp.where(j == 0, WINDOW, 0)
    bias_rest = jnp.where(band, 0.0, MASK_VALUE)
    bias_first = jnp.where(band & (kj >= first_lo), 0.0, MASK_VALUE)
    n_chunks = TILE // WINDOW
    bias2_first = jnp.concatenate([bias_first, bias_first], axis=1)
    bias2_rest = jnp.concatenate([bias_rest, bias_rest], axis=1)
    bias_tile = jnp.concatenate([bias2_first] + [bias2_rest] * (n_chunks - 1), axis=0)
    low_head_t = lax.broadcasted_iota(jnp.int32, (TILE, LANES), 1) < HEAD_DIM

    def attention_group(g, after):
        h = (g * 2) // (N_Q_HEADS // N_KV_HEADS)
        qg = rotary(qz[:, g * LANES:(g + 1) * LANES]) * (HEAD_DIM ** -0.5)
        if after is not None:
            qg = qg + after
        qg = qg.astype(BF16)
        windows = [slice(c * WINDOW, c * WINDOW + 2 * WINDOW) for c in range(n_chunks)]
        s = jnp.concatenate(
            [jnp.concatenate([_dot_nt(qg[c * WINDOW:(c + 1) * WINDOW, :], k_buf[2 * h + side, win, :])
                              for side in range(2)], axis=1)
             for c, win in enumerate(windows)], axis=0) + bias_tile
        probs = []
        inv = []
        for side in range(2):
            s_side = s[:, side * 2 * WINDOW:(side + 1) * 2 * WINDOW]
            sink = sink_ref[2 * g + side]
            m = jnp.maximum(jnp.max(s_side, axis=-1, keepdims=True), sink)
            p = jnp.exp(s_side - m)
            den = jnp.sum(p, axis=-1, keepdims=True) + jnp.exp(sink - m)
            inv.append(1.0 / den)
            probs.append(p.astype(BF16))
        out = jnp.concatenate(
            [_dot(probs[0][c * WINDOW:(c + 1) * WINDOW, :], v_buf[2 * h, win, :])
             + _dot(probs[1][c * WINDOW:(c + 1) * WINDOW, :], v_buf[2 * h + 1, win, :])
             for c, win in enumerate(windows)], axis=0)
        out = out * jnp.where(low_head_t, inv[0], inv[1])
        yc_s[:, g * LANES:(g + 1) * LANES] = out.astype(BF16)

    assert HALF // LANES == N_BRANCH and Q_WIDTH // LANES == N_BRANCH
    conv_cols = []
    gates = []
    after = None
    for n in range(N_BRANCH):
        half_z = _dot(xn_half, w_in_ref[:, OFF_G + n * D_MODEL:OFF_G + (n + 1) * D_MODEL])
        t = jnp.tanh(half_z)
        gates.append(t + 1.0)
        conv_cols.append(jnp.concatenate(
            [_causal_conv_block(cv_buf, cv_w_ref, CV_KERNEL, CV_HALO, r0, n * LANES, after)
             for r0 in range(0, TILE, CONV_ROWS)], axis=0))
        attention_group(n, after)
        after = _zero_after(t)
    for buf in (k_buf, v_buf):
        buf[:, 0:WINDOW, :] = buf[:, TILE:TILE + WINDOW, :]
    conv = jnp.concatenate(conv_cols, axis=1)
    cv_buf[0:CV_HALO, :] = cv_buf[TILE:TILE + CV_HALO, :]

    za = proj(OFF_A, 2 * HALF)
    yb = _layernorm(conv + (cv_b_ref[...] + after[:, 0:1]), cv_g_ref[...], cv_beta_ref[...])
    yb = (yb * _sigmoid(yb)).astype(BF16)

    zd = proj(OFF_D, 3 * HALF)
    ga = 0.5 * za * (1.0 + lax.erf(za * (1.0 / math.sqrt(2.0))))
    u = ga[:, :HALF]
    vln = _layernorm(ga[:, HALF:], sg_g_ref[...], sg_beta_ref[...]).astype(BF16)

    row = lax.broadcasted_iota(jnp.int32, (SG_CHUNK, SG_CHUNK), 0)
    col = lax.broadcasted_iota(jnp.int32, (SG_CHUNK, SG_CHUNK), 1)
    gw = SG_CHUNK
    for g in range(SG_GROUPS):
        w_g = jnp.where(row >= col, sg_w_ref[g], 0.0).astype(BF16)
        b_g = sg_bt_ref[:, g:g + 1]
        mixed = jnp.concatenate(
            [_dot(w_g, vln[r0:r0 + SG_CHUNK, g * gw:(g + 1) * gw]) + b_g
             for r0 in range(0, TILE, SG_CHUNK)], axis=0)
        ya_s[:, g * gw:(g + 1) * gw] = (u[:, g * gw:(g + 1) * gw] * mixed).astype(BF16)

    sc_buf[SC_HALO:SC_HALO + TILE, :] = zd[:, HALF:2 * HALF] * zd[:, 2 * HALF:]
    conv = jnp.concatenate(
        [jnp.concatenate([_causal_conv_block(sc_buf, sc_w_ref, SC_KERNEL, SC_HALO, r0, l0)
                          for r0 in range(0, TILE, CONV_ROWS)], axis=0)
         for l0 in range(0, HALF, LANES)], axis=1)
    sc_buf[0:SC_HALO, :] = sc_buf[TILE:TILE + SC_HALO, :]
    yd = (zd[:, :HALF] * conv).astype(BF16)

    merged = (gates[1] * _dot(yb, w_br_ref[1]) + gates[2] * _dot(yc_s[...], w_br_ref[2])
              + gates[0] * _dot(ya_s[...], w_br_ref[0]) + gates[3] * _dot(yd, w_br_ref[3]))
    o_ref[...] = x + 0.5 * _dot(merged.astype(BF16), w_out_ref[...])


def _ffn_kernel(x_ref, nf_ref, w_gu_hbm, w_dn_hbm, nfinal_ref, o_ref,
                w_gu_ref, w_dn_ref, stage_wide, stage_narrow, dma_sem, *, layer, final_norm):
    @pl.when((pl.program_id(0) == 0) & (pl.program_id(1) == 0))
    def _():
        _load_cast(w_gu_hbm.at[layer], w_gu_ref, stage_wide, dma_sem)
        _load_cast(w_dn_hbm.at[layer], w_dn_ref, stage_narrow, dma_sem)

    x = x_ref[...]
    hn = _rmsnorm(x, nf_ref[...]).astype(BF16)
    gate = _dot(hn, w_gu_ref[:, :D_FF])
    up = _dot(hn, w_gu_ref[:, D_FF:])
    h = (gate * _sigmoid(gate) * up).astype(BF16)
    y = x + _dot(h, w_dn_ref[...])
    if final_norm:
        y = _rmsnorm(y, nfinal_ref[...])
    o_ref[...] = y


def _resident(shape):
    return pl.BlockSpec(shape, lambda b, j: (0,) * len(shape), pipeline_mode=pl.Buffered(1))


def _mixer_call(layer, x, cos_t, sin_t, sinks, nm, w_in, sg_g, sg_beta, sg_w, sg_bt,
                cv_w, cv_b, cv_g, cv_beta, sc_w, w_br, w_out):
    batch, seq, _ = x.shape
    tile_spec = pl.BlockSpec((None, TILE, D_MODEL), lambda b, j: (b, j, 0))
    rope_spec = pl.BlockSpec((TILE, LANES), lambda b, j: (j, 0))
    in_specs = [
        pl.BlockSpec(memory_space=pltpu.SMEM),
        tile_spec, rope_spec, rope_spec,
        _resident((1, D_MODEL)),
        pl.BlockSpec(memory_space=pl.ANY),
        _resident((1, HALF)), _resident((1, HALF)),
        _resident((SG_GROUPS, SG_CHUNK, SG_CHUNK)), _resident((SG_CHUNK, SG_GROUPS)),
        _resident((CV_KERNEL, HALF)), _resident((1, HALF)), _resident((1, HALF)), _resident((1, HALF)),
        _resident((SC_KERNEL, HALF)),
        pl.BlockSpec(memory_space=pl.ANY),
        pl.BlockSpec(memory_space=pl.ANY),
    ]
    scratch = [
        pltpu.VMEM((CV_HALO + TILE, HALF), F32),
        pltpu.VMEM((SC_HALO + TILE, HALF), F32),
        pltpu.VMEM((4, WINDOW + TILE, LANES), BF16),
        pltpu.VMEM((4, WINDOW + TILE, LANES), BF16),
        pltpu.VMEM((TILE, HALF), BF16),
        pltpu.VMEM((TILE, HALF), BF16),
        pltpu.VMEM((D_MODEL, PROJ_WIDTH), BF16),
        pltpu.VMEM((N_BRANCH, HALF, D_MODEL), BF16),
        pltpu.VMEM((D_MODEL, D_MODEL), BF16),
        pltpu.VMEM((STAGE_SLOTS, STAGE_ROWS_WIDE, PROJ_WIDTH), F32),
        pltpu.VMEM((STAGE_SLOTS, STAGE_ROWS_NARROW, D_MODEL), F32),
        pltpu.SemaphoreType.DMA((STAGE_SLOTS,)),
    ]
    return pl.pallas_call(
        functools.partial(_mixer_kernel, layer=layer),
        grid=(batch, seq // TILE),
        in_specs=in_specs,
        out_specs=tile_spec,
        out_shape=jax.ShapeDtypeStruct(x.shape, F32),
        scratch_shapes=scratch,
        compiler_params=pltpu.CompilerParams(
            dimension_semantics=("arbitrary", "arbitrary"),
            vmem_limit_bytes=VMEM_LIMIT_BYTES),
        name="mixer",
    )(sinks, x, cos_t, sin_t, nm, w_in, sg_g, sg_beta, sg_w, sg_bt,
      cv_w, cv_b, cv_g, cv_beta, sc_w, w_br, w_out)


def _ffn_call(layer, x, nf, w_gu, w_dn, nfinal, final_norm):
    batch, seq, _ = x.shape
    tile_spec = pl.BlockSpec((None, TILE, D_MODEL), lambda b, j: (b, j, 0))
    hbm = pl.BlockSpec(memory_space=pl.ANY)
    return pl.pallas_call(
        functools.partial(_ffn_kernel, layer=layer, final_norm=final_norm),
        grid=(batch, seq // TILE),
        in_specs=[tile_spec, _resident((1, D_MODEL)), hbm, hbm, _resident((1, D_MODEL))],
        out_specs=tile_spec,
        out_shape=jax.ShapeDtypeStruct(x.shape, F32),
        scratch_shapes=[
            pltpu.VMEM((D_MODEL, 2 * D_FF), BF16),
            pltpu.VMEM((D_FF, D_MODEL), BF16),
            pltpu.VMEM((STAGE_SLOTS, STAGE_ROWS_WIDE, 2 * D_FF), F32),
            pltpu.VMEM((STAGE_SLOTS, STAGE_ROWS_NARROW, D_MODEL), F32),
            pltpu.SemaphoreType.DMA((STAGE_SLOTS,)),
        ],
        compiler_params=pltpu.CompilerParams(
            dimension_semantics=("arbitrary", "arbitrary"),
            vmem_limit_bytes=VMEM_LIMIT_BYTES),
        name="ffn",
    )(x, nf, w_gu, w_dn, nfinal)


def _rope_tables(seq):
    pos = jnp.arange(seq, dtype=F32)
    inv_freq = 1.0 / (ROPE_THETA ** (jnp.arange(0, HEAD_DIM, 2, dtype=F32) / HEAD_DIM))
    ang = pos[:, None] * inv_freq[None, :]
    cos = jnp.cos(ang)
    sin = jnp.sin(ang)
    reps = LANES // HEAD_DIM
    cos_t = jnp.tile(jnp.concatenate([cos, cos], axis=-1), (1, reps))
    sin_t = jnp.tile(jnp.concatenate([-sin, sin], axis=-1), (1, reps))
    return cos_t, sin_t


def kernel(x, norm_mix, w_in, sg_ln_g, sg_ln_b, sg_w, sg_b, cv_w, cv_b, cv_ln_g, cv_ln_b,
           attn_sinks, sc_w, w_branch, w_out, norm_ffn, w_gate_up, w_down, norm_final):
    depth = w_in.shape[0]
    seq = x.shape[1]
    assert x.shape[2] == D_MODEL and seq % TILE == 0 and TILE % WINDOW == 0
    cos_t, sin_t = _rope_tables(seq)
    row = lambda a: a.reshape(1, -1)
    for l in range(depth):
        x = _mixer_call(
            l, x, cos_t, sin_t, attn_sinks[l], row(norm_mix[l]), w_in,
            row(sg_ln_g[l]), row(sg_ln_b[l]), sg_w[l], sg_b[l].T,
            cv_w[l], row(cv_b[l]), row(cv_ln_g[l]), row(cv_ln_b[l]), sc_w[l],
            w_branch, w_out)
        x = _ffn_call(l, x, row(norm_ffn[l]), w_gate_up, w_down,
                      row(norm_final), final_norm=(l == depth - 1))
    return x
```

```python
import functools
import math

import jax
import jax.numpy as jnp
from jax import lax
from jax.experimental import pallas as pl
from jax.experimental.pallas import tpu as pltpu

D_MODEL = 1024
HALF = D_MODEL // 2
SG_CHUNK = 128
SG_GROUPS = 4
CV_KERNEL = 31
HEAD_DIM = 64
N_Q_HEADS = 8
N_KV_HEADS = 2
Q_WIDTH = N_Q_HEADS * HEAD_DIM
KV_WIDTH = N_KV_HEADS * HEAD_DIM
WINDOW = 128
ROPE_THETA = 10000.0
SC_KERNEL = 3
N_BRANCH = 4
D_FF = 2816
EPS = 1e-6

OFF_A = 0
OFF_B = OFF_A + 2 * HALF
OFF_Q = OFF_B + 2 * HALF
OFF_K = OFF_Q + Q_WIDTH
OFF_V = OFF_K + KV_WIDTH
OFF_D = OFF_V + KV_WIDTH
OFF_G = OFF_D + 3 * HALF
PROJ_WIDTH = OFF_G + N_BRANCH * D_MODEL

LANES = 128
SUBLANES = 8
TILE = 512
FFN_TILE = 1024
CV_HALO = 32
SC_HALO = SUBLANES
CONV_ROWS = 128
MASK_VALUE = -1e30
VMEM_LIMIT_BYTES = 60 * 1024 * 1024

F32 = jnp.float32
BF16 = jnp.bfloat16


def _sigmoid(x):
    return 0.5 * jnp.tanh(0.5 * x) + 0.5


def _rmsnorm(x, g):
    return x * lax.rsqrt(jnp.mean(x * x, axis=-1, keepdims=True) + EPS) * g


def _layernorm(x, g, b):
    mu = jnp.mean(x, axis=-1, keepdims=True)
    xc = x - mu
    var = jnp.mean(xc * xc, axis=-1, keepdims=True)
    return xc * lax.rsqrt(var + EPS) * g + b


def _dot(a, b):
    return jnp.dot(a, b, preferred_element_type=F32)


def _dot_nt(a, b):
    return lax.dot_general(a, b, (((1,), (1,)), ((), ())), preferred_element_type=F32)


def _zero_after(t):
    return t[-1:, -LANES:] * 0.0


def _causal_conv_block(buf, w_ref, ksize, halo, r0, l0, after=None):
    rows = CONV_ROWS + SUBLANES
    out = None
    for r in range(min(SUBLANES, ksize)):
        z = None
        for a in range((ksize - 1 - r) // SUBLANES + 1):
            tap = ksize - 1 - (SUBLANES * a + r)
            start = halo + r0 - SUBLANES * (a + 1)
            w_tap = w_ref[tap:tap + 1, l0:l0 + LANES]
            if after is not None:
                w_tap = w_tap + after
            term = w_tap * buf[start:start + rows, l0:l0 + LANES]
            z = term if z is None else z + term
        if r:
            z = pltpu.roll(z, r, 0)
        z = z[SUBLANES:, :]
        out = z if out is None else out + z
    return out


def _mixer_kernel(sink_ref, x_ref, cos_ref, sin_ref, nm_ref, w_in_ref,
                  sg_g_ref, sg_beta_ref, sg_w_ref, sg_bt_ref,
                  cv_w_ref, cv_b_ref, cv_g_ref, cv_beta_ref, sc_w_ref,
                  w_br_ref, w_out_ref, o_ref,
                  cv_buf, sc_buf, k_buf, v_buf, ya_s, yc_s):
    j = pl.program_id(1)

    @pl.when(j == 0)
    def _():
        cv_buf[0:CV_HALO, :] = jnp.zeros((CV_HALO, HALF), F32)
        sc_buf[0:SC_HALO, :] = jnp.zeros((SC_HALO, HALF), F32)
        k_buf[:, 0:WINDOW, :] = jnp.zeros((4, WINDOW, LANES), BF16)
        v_buf[:, 0:WINDOW, :] = jnp.zeros((4, WINDOW, LANES), BF16)

    x = x_ref[...]
    xn_f32 = _rmsnorm(x, nm_ref[...])
    xn = xn_f32.astype(BF16)
    xn_half = (0.5 * xn_f32).astype(BF16)

    def proj(off, width):
        return _dot(xn, w_in_ref[:, off:off + width])

    zb = proj(OFF_B, 2 * HALF)
    cv_buf[CV_HALO:CV_HALO + TILE, :] = zb[:, :HALF] * _sigmoid(zb[:, HALF:])

    cos = cos_ref[...]
    sin = sin_ref[...]
    lane = lax.broadcasted_iota(jnp.int32, (TILE, LANES), 1)
    first_half = (lane % HEAD_DIM) < (HEAD_DIM // 2)
    low_head = lane < HEAD_DIM

    def rotary(t):
        swapped = jnp.where(first_half,
                            pltpu.roll(t, LANES - HEAD_DIM // 2, 1),
                            pltpu.roll(t, HEAD_DIM // 2, 1))
        return t * cos + swapped * sin

    kv = proj(OFF_K, 2 * KV_WIDTH)
    kr = rotary(kv[:, :KV_WIDTH])
    vv = kv[:, KV_WIDTH:]
    for buf, t in ((k_buf, kr), (v_buf, vv)):
        sw = pltpu.roll(t, HEAD_DIM, 1)
        buf[0, WINDOW:WINDOW + TILE, :] = jnp.where(low_head, t, 0.0).astype(BF16)
        buf[1, WINDOW:WINDOW + TILE, :] = jnp.where(low_head, 0.0, sw).astype(BF16)
        buf[2, WINDOW:WINDOW + TILE, :] = jnp.where(low_head, sw, 0.0).astype(BF16)
        buf[3, WINDOW:WINDOW + TILE, :] = jnp.where(low_head, 0.0, t).astype(BF16)

    qz = proj(OFF_Q, Q_WIDTH)
    qi = lax.broadcasted_iota(jnp.int32, (WINDOW, 2 * WINDOW), 0)
    kj = lax.broadcasted_iota(jnp.int32, (WINDOW, 2 * WINDOW), 1)
    band = (kj > qi) & (kj <= qi + WINDOW)
    first_lo = jnp.where(j == 0, WINDOW, 0)
    bias_rest = jnp.where(band, 0.0, MASK_VALUE)
    bias_first = jnp.where(band & (kj >= first_lo), 0.0, MASK_VALUE)
    n_chunks = TILE // WINDOW
    bias2_first = jnp.concatenate([bias_first, bias_first], axis=1)
    bias2_rest = jnp.concatenate([bias_rest, bias_rest], axis=1)
    bias_tile = jnp.concatenate([bias2_first] + [bias2_rest] * (n_chunks - 1), axis=0)
    low_head_t = lax.broadcasted_iota(jnp.int32, (TILE, LANES), 1) < HEAD_DIM

    def attention_group(g, after):
        h = (g * 2) // (N_Q_HEADS // N_KV_HEADS)
        qg = rotary(qz[:, g * LANES:(g + 1) * LANES]) * (HEAD_DIM ** -0.5)
        if after is not None:
            qg = qg + after
        qg = qg.astype(BF16)
        windows = [slice(c * WINDOW, c * WINDOW + 2 * WINDOW) for c in range(n_chunks)]
        s = jnp.concatenate(
            [jnp.concatenate([_dot_nt(qg[c * WINDOW:(c + 1) * WINDOW, :], k_buf[2 * h + side, win, :])
                              for side in range(2)], axis=1)
             for c, win in enumerate(windows)], axis=0) + bias_tile
        probs = []
        inv = []
        for side in range(2):
            s_side = s[:, side * 2 * WINDOW:(side + 1) * 2 * WINDOW]
            sink = sink_ref[2 * g + side]
            m = jnp.maximum(jnp.max(s_side, axis=-1, keepdims=True), sink)
            p = jnp.exp(s_side - m)
            den = jnp.sum(p, axis=-1, keepdims=True) + jnp.exp(sink - m)
            inv.append(1.0 / den)
            probs.append(p.astype(BF16))
        out = jnp.concatenate(
            [_dot(probs[0][c * WINDOW:(c + 1) * WINDOW, :], v_buf[2 * h, win, :])
             + _dot(probs[1][c * WINDOW:(c + 1) * WINDOW, :], v_buf[2 * h + 1, win, :])
             for c, win in enumerate(windows)], axis=0)
        out = out * jnp.where(low_head_t, inv[0], inv[1])
        yc_s[:, g * LANES:(g + 1) * LANES] = out.astype(BF16)

    assert HALF // LANES == N_BRANCH and Q_WIDTH // LANES == N_BRANCH
    conv_cols = []
    gates = []
    after = None
    for n in range(N_BRANCH):
        half_z = _dot(xn_half, w_in_ref[:, OFF_G + n * D_MODEL:OFF_G + (n + 1) * D_MODEL])
        t = jnp.tanh(half_z)
        gates.append(t + 1.0)
        conv_cols.append(jnp.concatenate(
            [_causal_conv_block(cv_buf, cv_w_ref, CV_KERNEL, CV_HALO, r0, n * LANES, after)
             for r0 in range(0, TILE, CONV_ROWS)], axis=0))
        attention_group(n, after)
        after = _zero_after(t)
    for buf in (k_buf, v_buf):
        buf[:, 0:WINDOW, :] = buf[:, TILE:TILE + WINDOW, :]
    conv = jnp.concatenate(conv_cols, axis=1)
    cv_buf[0:CV_HALO, :] = cv_buf[TILE:TILE + CV_HALO, :]

    za = proj(OFF_A, 2 * HALF)
    yb = _layernorm(conv + (cv_b_ref[...] + after[:, 0:1]), cv_g_ref[...], cv_beta_ref[...])
    yb = (yb * _sigmoid(yb)).astype(BF16)

    zd = proj(OFF_D, 3 * HALF)
    ga = 0.5 * za * (1.0 + lax.erf(za * (1.0 / math.sqrt(2.0))))
    u = ga[:, :HALF]
    vln = _layernorm(ga[:, HALF:], sg_g_ref[...], sg_beta_ref[...]).astype(BF16)

    row = lax.broadcasted_iota(jnp.int32, (SG_CHUNK, SG_CHUNK), 0)
    col = lax.broadcasted_iota(jnp.int32, (SG_CHUNK, SG_CHUNK), 1)
    gw = SG_CHUNK
    for g in range(SG_GROUPS):
        w_g = jnp.where(row >= col, sg_w_ref[g], 0.0).astype(BF16)
        b_g = sg_bt_ref[:, g:g + 1]
        mixed = jnp.concatenate(
            [_dot(w_g, vln[r0:r0 + SG_CHUNK, g * gw:(g + 1) * gw]) + b_g
             for r0 in range(0, TILE, SG_CHUNK)], axis=0)
        ya_s[:, g * gw:(g + 1) * gw] = (u[:, g * gw:(g + 1) * gw] * mixed).astype(BF16)

    sc_buf[SC_HALO:SC_HALO + TILE, :] = zd[:, HALF:2 * HALF] * zd[:, 2 * HALF:]
    conv = jnp.concatenate(
        [jnp.concatenate([_causal_conv_block(sc_buf, sc_w_ref, SC_KERNEL, SC_HALO, r0, l0)
                          for r0 in range(0, TILE, CONV_ROWS)], axis=0)
         for l0 in range(0, HALF, LANES)], axis=1)
    sc_buf[0:SC_HALO, :] = sc_buf[TILE:TILE + SC_HALO, :]
    yd = (zd[:, :HALF] * conv).astype(BF16)

    merged = (gates[1] * _dot(yb, w_br_ref[1]) + gates[2] * _dot(yc_s[...], w_br_ref[2])
              + gates[0] * _dot(ya_s[...], w_br_ref[0]) + gates[3] * _dot(yd, w_br_ref[3]))
    o_ref[...] = x + 0.5 * _dot(merged.astype(BF16), w_out_ref[...])


def _ffn_kernel(x_ref, nf_ref, w_gu_ref, w_dn_ref, nfinal_ref, o_ref, *, final_norm):
    x = x_ref[...]
    hn = _rmsnorm(x, nf_ref[...]).astype(BF16)
    gate = _dot(hn, w_gu_ref[:, :D_FF])
    up = _dot(hn, w_gu_ref[:, D_FF:])
    h = (gate * _sigmoid(gate) * up).astype(BF16)
    y = x + _dot(h, w_dn_ref[...])
    if final_norm:
        y = _rmsnorm(y, nfinal_ref[...])
    o_ref[...] = y


def _resident(shape, layer=None):
    if layer is None:
        return pl.BlockSpec(shape, lambda b, j: (0,) * len(shape), pipeline_mode=pl.Buffered(1))
    return pl.BlockSpec((None,) + shape, lambda b, j: (layer,) + (0,) * len(shape),
                        pipeline_mode=pl.Buffered(1))


def _mixer_call(layer, x, cos_t, sin_t, sinks, nm, w_in, sg_g, sg_beta, sg_w, sg_bt,
                cv_w, cv_b, cv_g, cv_beta, sc_w, w_br, w_out):
    batch, seq, _ = x.shape
    tile_spec = pl.BlockSpec((None, TILE, D_MODEL), lambda b, j: (b, j, 0))
    rope_spec = pl.BlockSpec((TILE, LANES), lambda b, j: (j, 0))
    in_specs = [
        pl.BlockSpec(memory_space=pltpu.SMEM),
        tile_spec, rope_spec, rope_spec,
        _resident((1, D_MODEL)),
        _resident((D_MODEL, PROJ_WIDTH), layer),
        _resident((1, HALF)), _resident((1, HALF)),
        _resident((SG_GROUPS, SG_CHUNK, SG_CHUNK)), _resident((SG_CHUNK, SG_GROUPS)),
        _resident((CV_KERNEL, HALF)), _resident((1, HALF)), _resident((1, HALF)), _resident((1, HALF)),
        _resident((SC_KERNEL, HALF)),
        _resident((N_BRANCH, HALF, D_MODEL), layer),
        _resident((D_MODEL, D_MODEL), layer),
    ]
    scratch = [
        pltpu.VMEM((CV_HALO + TILE, HALF), F32),
        pltpu.VMEM((SC_HALO + TILE, HALF), F32),
        pltpu.VMEM((4, WINDOW + TILE, LANES), BF16),
        pltpu.VMEM((4, WINDOW + TILE, LANES), BF16),
        pltpu.VMEM((TILE, HALF), BF16),
        pltpu.VMEM((TILE, HALF), BF16),
    ]
    return pl.pallas_call(
        _mixer_kernel,
        grid=(batch, seq // TILE),
        in_specs=in_specs,
        out_specs=tile_spec,
        out_shape=jax.ShapeDtypeStruct(x.shape, F32),
        scratch_shapes=scratch,
        compiler_params=pltpu.CompilerParams(
            dimension_semantics=("arbitrary", "arbitrary"),
            vmem_limit_bytes=VMEM_LIMIT_BYTES),
        name="mixer",
    )(sinks, x, cos_t, sin_t, nm, w_in, sg_g, sg_beta, sg_w, sg_bt,
      cv_w, cv_b, cv_g, cv_beta, sc_w, w_br, w_out)


def _ffn_call(layer, x, nf, w_gu, w_dn, nfinal, final_norm):
    batch, seq, _ = x.shape
    tile_spec = pl.BlockSpec((None, FFN_TILE, D_MODEL), lambda b, j: (b, j, 0))
    return pl.pallas_call(
        functools.partial(_ffn_kernel, final_norm=final_norm),
        grid=(batch, seq // FFN_TILE),
        in_specs=[tile_spec, _resident((1, D_MODEL)), _resident((D_MODEL, 2 * D_FF), layer),
                  _resident((D_FF, D_MODEL), layer), _resident((1, D_MODEL))],
        out_specs=tile_spec,
        out_shape=jax.ShapeDtypeStruct(x.shape, F32),
        compiler_params=pltpu.CompilerParams(
            dimension_semantics=("arbitrary", "arbitrary"),
            vmem_limit_bytes=VMEM_LIMIT_BYTES),
        name="ffn",
    )(x, nf, w_gu, w_dn, nfinal)


def _rope_tables(seq):
    pos = jnp.arange(seq, dtype=F32)
    inv_freq = 1.0 / (ROPE_THETA ** (jnp.arange(0, HEAD_DIM, 2, dtype=F32) / HEAD_DIM))
    ang = pos[:, None] * inv_freq[None, :]
    cos = jnp.cos(ang)
    sin = jnp.sin(ang)
    reps = LANES // HEAD_DIM
    cos_t = jnp.tile(jnp.concatenate([cos, cos], axis=-1), (1, reps))
    sin_t = jnp.tile(jnp.concatenate([-sin, sin], axis=-1), (1, reps))
    return cos_t, sin_t


def kernel(x, norm_mix, w_in, sg_ln_g, sg_ln_b, sg_w, sg_b, cv_w, cv_b, cv_ln_g, cv_ln_b,
           attn_sinks, sc_w, w_branch, w_out, norm_ffn, w_gate_up, w_down, norm_final):
    depth = w_in.shape[0]
    seq = x.shape[1]
    assert x.shape[2] == D_MODEL and seq % TILE == 0 and TILE % WINDOW == 0 and seq % FFN_TILE == 0
    cos_t, sin_t = _rope_tables(seq)
    row = lambda a: a.reshape(1, -1)
    w_in, w_branch, w_out, w_gate_up, w_down = (
        w.astype(BF16) for w in (w_in, w_branch, w_out, w_gate_up, w_down))
    for l in range(depth):
        x = _mixer_call(
            l, x, cos_t, sin_t, attn_sinks[l], row(norm_mix[l]), w_in,
            row(sg_ln_g[l]), row(sg_ln_b[l]), sg_w[l], sg_b[l].T,
            cv_w[l], row(cv_b[l]), row(cv_ln_g[l]), row(cv_ln_b[l]), sc_w[l],
            w_branch, w_out)
        x = _ffn_call(l, x, row(norm_ffn[l]), w_gate_up, w_down,
                      row(norm_final), final_norm=(l == depth - 1))
    return x
```

```python
import functools
import math

import jax
import jax.numpy as jnp
from jax import lax
from jax.experimental import pallas as pl
from jax.experimental.pallas import tpu as pltpu

D_MODEL = 1024
HALF = D_MODEL // 2
SG_CHUNK = 128
SG_GROUPS = 4
CV_KERNEL = 31
HEAD_DIM = 64
N_Q_HEADS = 8
N_KV_HEADS = 2
Q_WIDTH = N_Q_HEADS * HEAD_DIM
KV_WIDTH = N_KV_HEADS * HEAD_DIM
WINDOW = 128
ROPE_THETA = 10000.0
SC_KERNEL = 3
N_BRANCH = 4
D_FF = 2816
EPS = 1e-6

OFF_A = 0
OFF_B = OFF_A + 2 * HALF
OFF_Q = OFF_B + 2 * HALF
OFF_K = OFF_Q + Q_WIDTH
OFF_V = OFF_K + KV_WIDTH
OFF_D = OFF_V + KV_WIDTH
OFF_G = OFF_D + 3 * HALF
PROJ_WIDTH = OFF_G + N_BRANCH * D_MODEL

LANES = 128
SUBLANES = 8
TILE = 512
FFN_TILE = 1024
CV_HALO = 32
SC_HALO = SUBLANES
CONV_ROWS = 128
MASK_VALUE = -1e30
VMEM_LIMIT_BYTES = 60 * 1024 * 1024

F32 = jnp.float32
BF16 = jnp.bfloat16


def _sigmoid(x):
    return 0.5 * jnp.tanh(0.5 * x) + 0.5


def _rmsnorm(x, g):
    return x * lax.rsqrt(jnp.mean(x * x, axis=-1, keepdims=True) + EPS) * g


def _layernorm(x, g, b):
    mu = jnp.mean(x, axis=-1, keepdims=True)
    xc = x - mu
    var = jnp.mean(xc * xc, axis=-1, keepdims=True)
    return xc * lax.rsqrt(var + EPS) * g + b


def _dot(a, b):
    return jnp.dot(a, b, preferred_element_type=F32)


def _dot_nt(a, b):
    return lax.dot_general(a, b, (((1,), (1,)), ((), ())), preferred_element_type=F32)


def _zero_after(t):
    return t[-1:, -LANES:] * 0.0


def _causal_conv_block(buf, w_ref, ksize, halo, r0, l0, after=None):
    rows = CONV_ROWS + SUBLANES
    out = None
    for r in range(min(SUBLANES, ksize)):
        z = None
        for a in range((ksize - 1 - r) // SUBLANES + 1):
            tap = ksize - 1 - (SUBLANES * a + r)
            start = halo + r0 - SUBLANES * (a + 1)
            w_tap = w_ref[tap:tap + 1, l0:l0 + LANES]
            if after is not None:
                w_tap = w_tap + after
            term = w_tap * buf[start:start + rows, l0:l0 + LANES]
            z = term if z is None else z + term
        if r:
            z = pltpu.roll(z, r, 0)
        z = z[SUBLANES:, :]
        out = z if out is None else out + z
    return out


def _mixer_kernel(sink_ref, x_ref, cos_ref, sin_ref, nm_ref, w_in_ref,
                  sg_g_ref, sg_beta_ref, sg_w_ref, sg_bt_ref,
                  cv_w_ref, cv_b_ref, cv_g_ref, cv_beta_ref, sc_w_ref,
                  w_br_ref, w_out_ref, o_ref,
                  cv_buf, sc_buf, k_buf, v_buf, ya_s, yc_s):
    j = pl.program_id(1)

    @pl.when(j == 0)
    def _():
        cv_buf[0:CV_HALO, :] = jnp.zeros((CV_HALO, HALF), F32)
        sc_buf[0:SC_HALO, :] = jnp.zeros((SC_HALO, HALF), F32)
        k_buf[:, 0:WINDOW, :] = jnp.zeros((4, WINDOW, LANES), BF16)
        v_buf[:, 0:WINDOW, :] = jnp.zeros((4, WINDOW, LANES), BF16)

    x = x_ref[...]
    xn_f32 = _rmsnorm(x, nm_ref[...])
    xn = xn_f32.astype(BF16)
    xn_half = (0.5 * xn_f32).astype(BF16)

    def proj(off, width):
        return _dot(xn, w_in_ref[:, off:off + width])

    zb_half = _dot(xn_half, w_in_ref[:, OFF_B:OFF_B + 2 * HALF])
    cv_buf[CV_HALO:CV_HALO + TILE, :] = zb_half[:, :HALF] * (jnp.tanh(zb_half[:, HALF:]) + 1.0)

    cos = cos_ref[...]
    sin = sin_ref[...]
    lane = lax.broadcasted_iota(jnp.int32, (TILE, LANES), 1)
    first_half = (lane % HEAD_DIM) < (HEAD_DIM // 2)
    low_head = lane < HEAD_DIM

    def rotary(t):
        swapped = jnp.where(first_half,
                            pltpu.roll(t, LANES - HEAD_DIM // 2, 1),
                            pltpu.roll(t, HEAD_DIM // 2, 1))
        return t * cos + swapped * sin

    kv = proj(OFF_K, 2 * KV_WIDTH)
    kr = rotary(kv[:, :KV_WIDTH]) * (HEAD_DIM ** -0.5)
    vv = kv[:, KV_WIDTH:]
    for buf, t in ((k_buf, kr), (v_buf, vv)):
        sw = pltpu.roll(t, HEAD_DIM, 1)
        buf[0, WINDOW:WINDOW + TILE, :] = jnp.where(low_head, t, 0.0).astype(BF16)
        buf[1, WINDOW:WINDOW + TILE, :] = jnp.where(low_head, 0.0, sw).astype(BF16)
        buf[2, WINDOW:WINDOW + TILE, :] = jnp.where(low_head, sw, 0.0).astype(BF16)
        buf[3, WINDOW:WINDOW + TILE, :] = jnp.where(low_head, 0.0, t).astype(BF16)

    qz = proj(OFF_Q, Q_WIDTH)
    qi = lax.broadcasted_iota(jnp.int32, (WINDOW, 2 * WINDOW), 0)
    kj = lax.broadcasted_iota(jnp.int32, (WINDOW, 2 * WINDOW), 1)
    band = (kj > qi) & (kj <= qi + WINDOW)
    first_lo = jnp.where(j == 0, WINDOW, 0)
    bias_rest = jnp.where(band, 0.0, MASK_VALUE)
    bias_first = jnp.where(band & (kj >= first_lo), 0.0, MASK_VALUE)
    n_chunks = TILE // WINDOW
    bias2_first = jnp.concatenate([bias_first, bias_first], axis=1)
    bias2_rest = jnp.concatenate([bias_rest, bias_rest], axis=1)
    bias_tile = jnp.concatenate([bias2_first] + [bias2_rest] * (n_chunks - 1), axis=0)
    low_head_t = lax.broadcasted_iota(jnp.int32, (TILE, LANES), 1) < HEAD_DIM

    def attention_group(g, after):
        h = (g * 2) // (N_Q_HEADS // N_KV_HEADS)
        qg = rotary(qz[:, g * LANES:(g + 1) * LANES])
        if after is not None:
            qg = qg + after
        qg = qg.astype(BF16)
        windows = [slice(c * WINDOW, c * WINDOW + 2 * WINDOW) for c in range(n_chunks)]
        s = jnp.concatenate(
            [jnp.concatenate([_dot_nt(qg[c * WINDOW:(c + 1) * WINDOW, :], k_buf[2 * h + side, win, :])
                              for side in range(2)], axis=1)
             for c, win in enumerate(windows)], axis=0) + bias_tile
        probs = []
        inv = []
        for side in range(2):
            s_side = s[:, side * 2 * WINDOW:(side + 1) * 2 * WINDOW]
            sink = sink_ref[2 * g + side]
            m = jnp.maximum(jnp.max(s_side, axis=-1, keepdims=True), sink)
            p = jnp.exp(s_side - m)
            den = jnp.sum(p, axis=-1, keepdims=True) + jnp.exp(sink - m)
            inv.append(1.0 / den)
            probs.append(p.astype(BF16))
        out = jnp.concatenate(
            [_dot(probs[0][c * WINDOW:(c + 1) * WINDOW, :], v_buf[2 * h, win, :])
             + _dot(probs[1][c * WINDOW:(c + 1) * WINDOW, :], v_buf[2 * h + 1, win, :])
             for c, win in enumerate(windows)], axis=0)
        out = out * jnp.where(low_head_t, inv[0], inv[1])
        yc_s[:, g * LANES:(g + 1) * LANES] = out.astype(BF16)

    assert HALF // LANES == N_BRANCH and Q_WIDTH // LANES == N_BRANCH
    conv_cols = []
    gates = []
    after = None
    for n in range(N_BRANCH):
        half_z = _dot(xn_half, w_in_ref[:, OFF_G + n * D_MODEL:OFF_G + (n + 1) * D_MODEL])
        t = jnp.tanh(half_z)
        gates.append(t + 1.0)
        conv_cols.append(jnp.concatenate(
            [_causal_conv_block(cv_buf, cv_w_ref, CV_KERNEL, CV_HALO, r0, n * LANES, after)
             for r0 in range(0, TILE, CONV_ROWS)], axis=0))
        attention_group(n, after)
        after = _zero_after(t)
    for buf in (k_buf, v_buf):
        buf[:, 0:WINDOW, :] = buf[:, TILE:TILE + WINDOW, :]
    conv = jnp.concatenate(conv_cols, axis=1)
    cv_buf[0:CV_HALO, :] = cv_buf[TILE:TILE + CV_HALO, :]

    za_half = _dot(xn_half, w_in_ref[:, OFF_A:OFF_A + 2 * HALF])
    yb_half = _layernorm(conv + (cv_b_ref[...] + after[:, 0:1]),
                         0.5 * cv_g_ref[...], 0.5 * cv_beta_ref[...])
    yb = (yb_half * (jnp.tanh(yb_half) + 1.0)).astype(BF16)

    zd = proj(OFF_D, 3 * HALF)
    ga = za_half * (1.0 + lax.erf(za_half * math.sqrt(2.0)))
    u = ga[:, :HALF]
    vln = _layernorm(ga[:, HALF:], sg_g_ref[...], sg_beta_ref[...]).astype(BF16)

    row = lax.broadcasted_iota(jnp.int32, (SG_CHUNK, SG_CHUNK), 0)
    col = lax.broadcasted_iota(jnp.int32, (SG_CHUNK, SG_CHUNK), 1)
    gw = SG_CHUNK
    for g in range(SG_GROUPS):
        w_g = jnp.where(row >= col, sg_w_ref[g], 0.0).astype(BF16)
        b_g = sg_bt_ref[:, g:g + 1]
        v_wide = jnp.concatenate(
            [vln[r0:r0 + SG_CHUNK, g * gw:(g + 1) * gw] for r0 in range(0, TILE, SG_CHUNK)], axis=1)
        mixed_wide = _dot(w_g, v_wide) + b_g
        mixed = jnp.concatenate(
            [mixed_wide[:, c * gw:(c + 1) * gw] for c in range(TILE // SG_CHUNK)], axis=0)
        ya_s[:, g * gw:(g + 1) * gw] = (u[:, g * gw:(g + 1) * gw] * mixed).astype(BF16)

    sc_buf[SC_HALO:SC_HALO + TILE, :] = zd[:, HALF:2 * HALF] * zd[:, 2 * HALF:]
    conv = jnp.concatenate(
        [jnp.concatenate([_causal_conv_block(sc_buf, sc_w_ref, SC_KERNEL, SC_HALO, r0, l0)
                          for r0 in range(0, TILE, CONV_ROWS)], axis=0)
         for l0 in range(0, HALF, LANES)], axis=1)
    sc_buf[0:SC_HALO, :] = sc_buf[TILE:TILE + SC_HALO, :]
    yd = (zd[:, :HALF] * conv).astype(BF16)

    merged = (gates[1] * _dot(yb, w_br_ref[1]) + gates[2] * _dot(yc_s[...], w_br_ref[2])
              + gates[0] * _dot(ya_s[...], w_br_ref[0]) + gates[3] * _dot(yd, w_br_ref[3]))
    o_ref[...] = x + 0.5 * _dot(merged.astype(BF16), w_out_ref[...])


def _ffn_kernel(x_ref, nf_ref, w_gu_ref, w_dn_ref, nfinal_ref, o_ref, *, final_norm):
    x = x_ref[...]
    hn = _rmsnorm(x, nf_ref[...]).astype(BF16)
    gate = _dot(hn, w_gu_ref[:, :D_FF])
    up = _dot(hn, w_gu_ref[:, D_FF:])
    h = (gate * _sigmoid(gate) * up).astype(BF16)
    y = x + _dot(h, w_dn_ref[...])
    if final_norm:
        y = _rmsnorm(y, nfinal_ref[...])
    o_ref[...] = y


def _resident(shape, layer=None):
    if layer is None:
        return pl.BlockSpec(shape, lambda b, j: (0,) * len(shape), pipeline_mode=pl.Buffered(1))
    return pl.BlockSpec((None,) + shape, lambda b, j: (layer,) + (0,) * len(shape),
                        pipeline_mode=pl.Buffered(1))


def _mixer_call(layer, x, cos_t, sin_t, sinks, nm, w_in, sg_g, sg_beta, sg_w, sg_bt,
                cv_w, cv_b, cv_g, cv_beta, sc_w, w_br, w_out):
    batch, seq, _ = x.shape
    tile_spec = pl.BlockSpec((None, TILE, D_MODEL), lambda b, j: (b, j, 0))
    rope_spec = pl.BlockSpec((TILE, LANES), lambda b, j: (j, 0))
    in_specs = [
        pl.BlockSpec(memory_space=pltpu.SMEM),
        tile_spec, rope_spec, rope_spec,
        _resident((1, D_MODEL)),
        _resident((D_MODEL, PROJ_WIDTH), layer),
        _resident((1, HALF)), _resident((1, HALF)),
        _resident((SG_GROUPS, SG_CHUNK, SG_CHUNK)), _resident((SG_CHUNK, SG_GROUPS)),
        _resident((CV_KERNEL, HALF)), _resident((1, HALF)), _resident((1, HALF)), _resident((1, HALF)),
        _resident((SC_KERNEL, HALF)),
        _resident((N_BRANCH, HALF, D_MODEL), layer),
        _resident((D_MODEL, D_MODEL), layer),
    ]
    scratch = [
        pltpu.VMEM((CV_HALO + TILE, HALF), F32),
        pltpu.VMEM((SC_HALO + TILE, HALF), F32),
        pltpu.VMEM((4, WINDOW + TILE, LANES), BF16),
        pltpu.VMEM((4, WINDOW + TILE, LANES), BF16),
        pltpu.VMEM((TILE, HALF), BF16),
        pltpu.VMEM((TILE, HALF), BF16),
    ]
    return pl.pallas_call(
        _mixer_kernel,
        grid=(batch, seq // TILE),
        in_specs=in_specs,
        out_specs=tile_spec,
        out_shape=jax.ShapeDtypeStruct(x.shape, F32),
        scratch_shapes=scratch,
        compiler_params=pltpu.CompilerParams(
            dimension_semantics=("arbitrary", "arbitrary"),
            vmem_limit_bytes=VMEM_LIMIT_BYTES),
        name="mixer",
    )(sinks, x, cos_t, sin_t, nm, w_in, sg_g, sg_beta, sg_w, sg_bt,
      cv_w, cv_b, cv_g, cv_beta, sc_w, w_br, w_out)


def _ffn_call(layer, x, nf, w_gu, w_dn, nfinal, final_norm):
    batch, seq, _ = x.shape
    tile_spec = pl.BlockSpec((None, FFN_TILE, D_MODEL), lambda b, j: (b, j, 0))
    return pl.pallas_call(
        functools.partial(_ffn_kernel, final_norm=final_norm),
        grid=(batch, seq // FFN_TILE),
        in_specs=[tile_spec, _resident((1, D_MODEL)), _resident((D_MODEL, 2 * D_FF), layer),
                  _resident((D_FF, D_MODEL), layer), _resident((1, D_MODEL))],
        out_specs=tile_spec,
        out_shape=jax.ShapeDtypeStruct(x.shape, F32),
        compiler_params=pltpu.CompilerParams(
            dimension_semantics=("arbitrary", "arbitrary"),
            vmem_limit_bytes=VMEM_LIMIT_BYTES),
        name="ffn",
    )(x, nf, w_gu, w_dn, nfinal)


def _rope_tables(seq):
    pos = jnp.arange(seq, dtype=F32)
    inv_freq = 1.0 / (ROPE_THETA ** (jnp.arange(0, HEAD_DIM, 2, dtype=F32) / HEAD_DIM))
    ang = pos[:, None] * inv_freq[None, :]
    cos = jnp.cos(ang)
    sin = jnp.sin(ang)
    reps = LANES // HEAD_DIM
    cos_t = jnp.tile(jnp.concatenate([cos, cos], axis=-1), (1, reps))
    sin_t = jnp.tile(jnp.concatenate([-sin, sin], axis=-1), (1, reps))
    return cos_t, sin_t


def kernel(x, norm_mix, w_in, sg_ln_g, sg_ln_b, sg_w, sg_b, cv_w, cv_b, cv_ln_g, cv_ln_b,
           attn_sinks, sc_w, w_branch, w_out, norm_ffn, w_gate_up, w_down, norm_final):
    depth = w_in.shape[0]
    seq = x.shape[1]
    assert x.shape[2] == D_MODEL and seq % TILE == 0 and TILE % WINDOW == 0 and seq % FFN_TILE == 0
    cos_t, sin_t = _rope_tables(seq)
    row = lambda a: a.reshape(1, -1)
    w_in, w_branch, w_out, w_gate_up, w_down = (
        w.astype(BF16) for w in (w_in, w_branch, w_out, w_gate_up, w_down))
    for l in range(depth):
        x = _mixer_call(
            l, x, cos_t, sin_t, attn_sinks[l], row(norm_mix[l]), w_in,
            row(sg_ln_g[l]), row(sg_ln_b[l]), sg_w[l], sg_b[l].T,
            cv_w[l], row(cv_b[l]), row(cv_ln_g[l]), row(cv_ln_b[l]), sc_w[l],
            w_branch, w_out)
        x = _ffn_call(l, x, row(norm_ffn[l]), w_gate_up, w_down,
                      row(norm_final), final_norm=(l == depth - 1))
    return x
```

```python
import functools
import math

import jax
import jax.numpy as jnp
from jax import lax
from jax.experimental import pallas as pl
from jax.experimental.pallas import tpu as pltpu

D_MODEL = 1024
HALF = D_MODEL // 2
SG_CHUNK = 128
SG_GROUPS = 4
CV_KERNEL = 31
HEAD_DIM = 64
N_Q_HEADS = 8
N_KV_HEADS = 2
Q_WIDTH = N_Q_HEADS * HEAD_DIM
KV_WIDTH = N_KV_HEADS * HEAD_DIM
WINDOW = 128
ROPE_THETA = 10000.0
SC_KERNEL = 3
N_BRANCH = 4
D_FF = 2816
EPS = 1e-6

OFF_A = 0
OFF_B = OFF_A + 2 * HALF
OFF_Q = OFF_B + 2 * HALF
OFF_K = OFF_Q + Q_WIDTH
OFF_V = OFF_K + KV_WIDTH
OFF_D = OFF_V + KV_WIDTH
OFF_G = OFF_D + 3 * HALF
PROJ_WIDTH = OFF_G + N_BRANCH * D_MODEL

LANES = 128
SUBLANES = 8
BF16_SUBLANES = 16
TILE = 512
FFN_TILE = 1024
CV_HALO = 32
SC_HALO = SUBLANES
CONV_ROWS = 128
MASK_VALUE = -1e30
VMEM_LIMIT_BYTES = 60 * 1024 * 1024

F32 = jnp.float32
BF16 = jnp.bfloat16


def _sigmoid(x):
    return 0.5 * jnp.tanh(0.5 * x) + 0.5


def _rmsnorm(x, g):
    return x * lax.rsqrt(jnp.mean(x * x, axis=-1, keepdims=True) + EPS) * g


def _layernorm(x, g, b):
    mu = jnp.mean(x, axis=-1, keepdims=True)
    xc = x - mu
    var = jnp.mean(xc * xc, axis=-1, keepdims=True)
    return xc * lax.rsqrt(var + EPS) * g + b


def _dot(a, b):
    return jnp.dot(a, b, preferred_element_type=F32)


def _dot_nt(a, b):
    return lax.dot_general(a, b, (((1,), (1,)), ((), ())), preferred_element_type=F32)


def _zero_after(t):
    return t[-1:, -LANES:] * 0.0


def _causal_conv_block(buf, w_ref, ksize, halo, r0, l0, after=None):
    rows = CONV_ROWS + SUBLANES
    out = None
    for r in range(min(SUBLANES, ksize)):
        z = None
        for a in range((ksize - 1 - r) // SUBLANES + 1):
            tap = ksize - 1 - (SUBLANES * a + r)
            start = halo + r0 - SUBLANES * (a + 1)
            w_tap = w_ref[tap:tap + 1, l0:l0 + LANES]
            if after is not None:
                w_tap = w_tap + after
            term = w_tap * buf[start:start + rows, l0:l0 + LANES]
            z = term if z is None else z + term
        if r:
            z = pltpu.roll(z, r, 0)
        z = z[SUBLANES:, :]
        out = z if out is None else out + z
    return out


def _mixer_kernel(sink_ref, x_ref, cos_ref, sin_ref, nm_ref, w_in_ref,
                  sg_g_ref, sg_beta_ref, sg_w_ref, sg_bt_ref,
                  cv_w_ref, cv_b_ref, cv_g_ref, cv_beta_ref, sc_w_ref,
                  w_br_ref, w_out_ref, o_ref,
                  cv_buf, sc_buf, k_buf, v_buf, ya_s, yc_s):
    j = pl.program_id(1)

    @pl.when(j == 0)
    def _():
        cv_buf[0:CV_HALO, :] = jnp.zeros((CV_HALO, HALF), F32)
        sc_buf[0:SC_HALO, :] = jnp.zeros((SC_HALO, HALF), F32)
        k_buf[:, 0:WINDOW, :] = jnp.zeros((4, WINDOW, LANES), BF16)
        v_buf[:, 0:WINDOW, :] = jnp.zeros((4, WINDOW, LANES), BF16)

    x = x_ref[...]
    xn_f32 = _rmsnorm(x, nm_ref[...])
    xn = xn_f32.astype(BF16)
    xn_half = (0.5 * xn_f32).astype(BF16)

    def proj(off, width):
        return _dot(xn, w_in_ref[:, off:off + width])

    zb_half = _dot(xn_half, w_in_ref[:, OFF_B:OFF_B + 2 * HALF])
    cv_buf[CV_HALO:CV_HALO + TILE, :] = zb_half[:, :HALF] * (jnp.tanh(zb_half[:, HALF:]) + 1.0)

    cos = cos_ref[...]
    sin = sin_ref[...]
    lane = lax.broadcasted_iota(jnp.int32, (TILE, LANES), 1)
    first_half = (lane % HEAD_DIM) < (HEAD_DIM // 2)
    low_head = lane < HEAD_DIM

    def rotary(t):
        swapped = jnp.where(first_half,
                            pltpu.roll(t, LANES - HEAD_DIM // 2, 1),
                            pltpu.roll(t, HEAD_DIM // 2, 1))
        return t * cos + swapped * sin

    kv = proj(OFF_K, 2 * KV_WIDTH)
    kr = rotary(kv[:, :KV_WIDTH]) * (HEAD_DIM ** -0.5)
    vv = kv[:, KV_WIDTH:]
    for buf, t in ((k_buf, kr), (v_buf, vv)):
        sw = pltpu.roll(t, HEAD_DIM, 1)
        buf[0, WINDOW:WINDOW + TILE, :] = jnp.where(low_head, t, 0.0).astype(BF16)
        buf[1, WINDOW:WINDOW + TILE, :] = jnp.where(low_head, 0.0, sw).astype(BF16)
        buf[2, WINDOW:WINDOW + TILE, :] = jnp.where(low_head, sw, 0.0).astype(BF16)
        buf[3, WINDOW:WINDOW + TILE, :] = jnp.where(low_head, 0.0, t).astype(BF16)

    qz = proj(OFF_Q, Q_WIDTH)
    qi = lax.broadcasted_iota(jnp.int32, (WINDOW, 2 * WINDOW), 0)
    kj = lax.broadcasted_iota(jnp.int32, (WINDOW, 2 * WINDOW), 1)
    band = (kj > qi) & (kj <= qi + WINDOW)
    first_lo = jnp.where(j == 0, WINDOW, 0)
    bias_rest = jnp.where(band, 0.0, MASK_VALUE)
    bias_first = jnp.where(band & (kj >= first_lo), 0.0, MASK_VALUE)
    n_chunks = TILE // WINDOW
    bias2_first = jnp.concatenate([bias_first, bias_first], axis=1)
    bias2_rest = jnp.concatenate([bias_rest, bias_rest], axis=1)
    bias_tile = jnp.concatenate([bias2_first] + [bias2_rest] * (n_chunks - 1), axis=0)
    low_head_t = lax.broadcasted_iota(jnp.int32, (TILE, LANES), 1) < HEAD_DIM

    def attention_group(g, after):
        h = (g * 2) // (N_Q_HEADS // N_KV_HEADS)
        qg = rotary(qz[:, g * LANES:(g + 1) * LANES])
        if after is not None:
            qg = qg + after
        qg = qg.astype(BF16)
        windows = [slice(c * WINDOW, c * WINDOW + 2 * WINDOW) for c in range(n_chunks)]
        s = jnp.concatenate(
            [jnp.concatenate([_dot_nt(qg[c * WINDOW:(c + 1) * WINDOW, :], k_buf[2 * h + side, win, :])
                              for side in range(2)], axis=1)
             for c, win in enumerate(windows)], axis=0) + bias_tile
        probs = []
        inv = []
        for side in range(2):
            s_side = s[:, side * 2 * WINDOW:(side + 1) * 2 * WINDOW]
            sink = sink_ref[2 * g + side]
            m = jnp.maximum(jnp.max(s_side, axis=-1, keepdims=True), sink)
            p = jnp.exp(s_side - m)
            den = jnp.sum(p, axis=-1, keepdims=True) + jnp.exp(sink - m)
            inv.append(1.0 / den)
            probs.append(p.astype(BF16))
        out = jnp.concatenate(
            [_dot(probs[0][c * WINDOW:(c + 1) * WINDOW, :], v_buf[2 * h, win, :])
             + _dot(probs[1][c * WINDOW:(c + 1) * WINDOW, :], v_buf[2 * h + 1, win, :])
             for c, win in enumerate(windows)], axis=0)
        out = out * jnp.where(low_head_t, inv[0], inv[1])
        yc_s[:, g * LANES:(g + 1) * LANES] = out.astype(BF16)

    assert HALF // LANES == N_BRANCH and Q_WIDTH // LANES == N_BRANCH
    conv_cols = []
    gates = []
    after = None
    for n in range(N_BRANCH):
        half_z = _dot(xn_half, w_in_ref[:, OFF_G + n * D_MODEL:OFF_G + (n + 1) * D_MODEL])
        t = jnp.tanh(half_z)
        gates.append(t + 1.0)
        conv_cols.append(jnp.concatenate(
            [_causal_conv_block(cv_buf, cv_w_ref, CV_KERNEL, CV_HALO, r0, n * LANES, after)
             for r0 in range(0, TILE, CONV_ROWS)], axis=0))
        attention_group(n, after)
        after = _zero_after(t)
    for buf in (k_buf, v_buf):
        buf[:, 0:WINDOW, :] = buf[:, TILE:TILE + WINDOW, :]
    conv = jnp.concatenate(conv_cols, axis=1)
    cv_buf[0:CV_HALO, :] = cv_buf[TILE:TILE + CV_HALO, :]

    za_half = _dot(xn_half, w_in_ref[:, OFF_A:OFF_A + 2 * HALF])
    yb_half = _layernorm(conv + (cv_b_ref[...] + after[:, 0:1]),
                         0.5 * cv_g_ref[...], 0.5 * cv_beta_ref[...])
    yb = (yb_half * (jnp.tanh(yb_half) + 1.0)).astype(BF16)

    zd = proj(OFF_D, 3 * HALF)
    ga = za_half * (1.0 + lax.erf(za_half * math.sqrt(2.0)))
    u = ga[:, :HALF]
    vln = _layernorm(ga[:, HALF:], sg_g_ref[...], sg_beta_ref[...]).astype(BF16)

    row = lax.broadcasted_iota(jnp.int32, (SG_CHUNK, SG_CHUNK), 0)
    col = lax.broadcasted_iota(jnp.int32, (SG_CHUNK, SG_CHUNK), 1)
    gw = SG_CHUNK
    for g in range(SG_GROUPS):
        w_g = jnp.where(row >= col, sg_w_ref[g], 0.0).astype(BF16)
        b_g = sg_bt_ref[:, g:g + 1]
        v_wide = jnp.concatenate(
            [vln[r0:r0 + SG_CHUNK, g * gw:(g + 1) * gw] for r0 in range(0, TILE, SG_CHUNK)], axis=1)
        mixed_wide = _dot(w_g, v_wide) + b_g
        mixed = jnp.concatenate(
            [mixed_wide[:, c * gw:(c + 1) * gw] for c in range(TILE // SG_CHUNK)], axis=0)
        ya_s[:, g * gw:(g + 1) * gw] = (u[:, g * gw:(g + 1) * gw] * mixed).astype(BF16)

    sc_buf[SC_HALO:SC_HALO + TILE, :] = zd[:, HALF:2 * HALF] * zd[:, 2 * HALF:]
    conv = jnp.concatenate(
        [jnp.concatenate([_causal_conv_block(sc_buf, sc_w_ref, SC_KERNEL, SC_HALO, r0, l0)
                          for r0 in range(0, TILE, CONV_ROWS)], axis=0)
         for l0 in range(0, HALF, LANES)], axis=1)
    sc_buf[0:SC_HALO, :] = sc_buf[TILE:TILE + SC_HALO, :]
    yd = (zd[:, :HALF] * conv).astype(BF16)

    merged = (gates[1] * _dot(yb, w_br_ref[1]) + gates[2] * _dot(yc_s[...], w_br_ref[2])
              + gates[0] * _dot(ya_s[...], w_br_ref[0]) + gates[3] * _dot(yd, w_br_ref[3]))
    o_ref[...] = x + 0.5 * _dot(merged.astype(BF16), w_out_ref[...])


def _ffn_kernel(*refs, final_norm, n_cast):
    x_ref, nf_ref, w_gu_ref, w_dn_ref, nfinal_ref = refs[:5]
    cast_src = refs[5:5 + n_cast]
    o_ref = refs[5 + n_cast]
    cast_dst = refs[6 + n_cast:]
    x = x_ref[...]
    hn = _rmsnorm(x, nf_ref[...]).astype(BF16)
    gate = _dot(hn, w_gu_ref[:, :D_FF])
    up = _dot(hn, w_gu_ref[:, D_FF:])
    h = (gate * _sigmoid(gate) * up).astype(BF16)
    y = x + _dot(h, w_dn_ref[...])
    if final_norm:
        y = _rmsnorm(y, nfinal_ref[...])
    o_ref[...] = y
    for src, dst in zip(cast_src, cast_dst):
        dst[...] = src[...].astype(BF16)


def _resident(shape):
    return pl.BlockSpec(shape, lambda b, j: (0,) * len(shape), pipeline_mode=pl.Buffered(1))


def _cast_rows(total_rows, n_steps):
    for rows in range(BF16_SUBLANES, total_rows + 1, BF16_SUBLANES):
        if total_rows % rows == 0 and rows * n_steps >= total_rows:
            return rows
    raise ValueError((total_rows, n_steps))


def _mixer_call(x, cos_t, sin_t, sinks, nm, w_in, sg_g, sg_beta, sg_w, sg_bt,
                cv_w, cv_b, cv_g, cv_beta, sc_w, w_br, w_out):
    batch, seq, _ = x.shape
    tile_spec = pl.BlockSpec((None, TILE, D_MODEL), lambda b, j: (b, j, 0))
    rope_spec = pl.BlockSpec((TILE, LANES), lambda b, j: (j, 0))
    in_specs = [
        pl.BlockSpec(memory_space=pltpu.SMEM),
        tile_spec, rope_spec, rope_spec,
        _resident((1, D_MODEL)),
        _resident((D_MODEL, PROJ_WIDTH)),
        _resident((1, HALF)), _resident((1, HALF)),
        _resident((SG_GROUPS, SG_CHUNK, SG_CHUNK)), _resident((SG_CHUNK, SG_GROUPS)),
        _resident((CV_KERNEL, HALF)), _resident((1, HALF)), _resident((1, HALF)), _resident((1, HALF)),
        _resident((SC_KERNEL, HALF)),
        _resident((N_BRANCH, HALF, D_MODEL)),
        _resident((D_MODEL, D_MODEL)),
    ]
    scratch = [
        pltpu.VMEM((CV_HALO + TILE, HALF), F32),
        pltpu.VMEM((SC_HALO + TILE, HALF), F32),
        pltpu.VMEM((4, WINDOW + TILE, LANES), BF16),
        pltpu.VMEM((4, WINDOW + TILE, LANES), BF16),
        pltpu.VMEM((TILE, HALF), BF16),
        pltpu.VMEM((TILE, HALF), BF16),
    ]
    return pl.pallas_call(
        _mixer_kernel,
        grid=(batch, seq // TILE),
        in_specs=in_specs,
        out_specs=tile_spec,
        out_shape=jax.ShapeDtypeStruct(x.shape, F32),
        scratch_shapes=scratch,
        compiler_params=pltpu.CompilerParams(
            dimension_semantics=("arbitrary", "arbitrary"),
            vmem_limit_bytes=VMEM_LIMIT_BYTES),
        name="mixer",
    )(sinks, x, cos_t, sin_t, nm, w_in, sg_g, sg_beta, sg_w, sg_bt,
      cv_w, cv_b, cv_g, cv_beta, sc_w, w_br, w_out)


def _ffn_call(x, nf, w_gu, w_dn, nfinal, final_norm, tile, cast_next=(), next_layer=0):
    batch, seq, _ = x.shape
    tiles_per_seq = seq // tile
    n_steps = batch * tiles_per_seq
    tile_spec = pl.BlockSpec((None, tile, D_MODEL), lambda b, j: (b, j, 0))
    src_specs, dst_specs, dst_shapes = [], [], []
    for w in cast_next:
        _, total_rows, cols = w.shape
        rows = _cast_rows(total_rows, n_steps)
        last = total_rows // rows - 1

        def block(b, j, last=last):
            return jnp.minimum(b * tiles_per_seq + j, last)
        src_specs.append(pl.BlockSpec((None, rows, cols), lambda b, j, block=block: (next_layer, block(b, j), 0)))
        dst_specs.append(pl.BlockSpec((rows, cols), lambda b, j, block=block: (block(b, j), 0)))
        dst_shapes.append(jax.ShapeDtypeStruct((total_rows, cols), BF16))
    out = pl.pallas_call(
        functools.partial(_ffn_kernel, final_norm=final_norm, n_cast=len(cast_next)),
        grid=(batch, tiles_per_seq),
        in_specs=[tile_spec, _resident((1, D_MODEL)), _resident((D_MODEL, 2 * D_FF)),
                  _resident((D_FF, D_MODEL)), _resident((1, D_MODEL))] + src_specs,
        out_specs=[tile_spec] + dst_specs,
        out_shape=[jax.ShapeDtypeStruct(x.shape, F32)] + dst_shapes,
        compiler_params=pltpu.CompilerParams(
            dimension_semantics=("arbitrary", "arbitrary"),
            vmem_limit_bytes=VMEM_LIMIT_BYTES),
        name="ffn",
    )(x, nf, w_gu, w_dn, nfinal, *cast_next)
    return out[0], tuple(out[1:])


def _rope_tables(seq):
    pos = jnp.arange(seq, dtype=F32)
    inv_freq = 1.0 / (ROPE_THETA ** (jnp.arange(0, HEAD_DIM, 2, dtype=F32) / HEAD_DIM))
    ang = pos[:, None] * inv_freq[None, :]
    cos = jnp.cos(ang)
    sin = jnp.sin(ang)
    reps = LANES // HEAD_DIM
    cos_t = jnp.tile(jnp.concatenate([cos, cos], axis=-1), (1, reps))
    sin_t = jnp.tile(jnp.concatenate([-sin, sin], axis=-1), (1, reps))
    return cos_t, sin_t


def kernel(x, norm_mix, w_in, sg_ln_g, sg_ln_b, sg_w, sg_b, cv_w, cv_b, cv_ln_g, cv_ln_b,
           attn_sinks, sc_w, w_branch, w_out, norm_ffn, w_gate_up, w_down, norm_final):
    depth = w_in.shape[0]
    seq = x.shape[1]
    assert x.shape[2] == D_MODEL and seq % TILE == 0 and TILE % WINDOW == 0 and seq % FFN_TILE == 0
    cos_t, sin_t = _rope_tables(seq)
    row = lambda a: a.reshape(1, -1)
    stacked = (w_in, w_branch.reshape(depth, N_BRANCH * HALF, D_MODEL), w_out, w_gate_up, w_down)
    weights = tuple(w[0].astype(BF16) for w in stacked)
    for l in range(depth):
        w_in_l, w_br_l, w_out_l, w_gu_l, w_dn_l = weights
        x = _mixer_call(
            x, cos_t, sin_t, attn_sinks[l], row(norm_mix[l]), w_in_l,
            row(sg_ln_g[l]), row(sg_ln_b[l]), sg_w[l], sg_b[l].T,
            cv_w[l], row(cv_b[l]), row(cv_ln_g[l]), row(cv_ln_b[l]), sc_w[l],
            w_br_l.reshape(N_BRANCH, HALF, D_MODEL), w_out_l)
        last = l == depth - 1
        x, weights = _ffn_call(
            x, row(norm_ffn[l]), w_gu_l, w_dn_l, row(norm_final), final_norm=last,
            tile=FFN_TILE if last else TILE, cast_next=() if last else stacked, next_layer=l + 1)
    return x
```

```python
import functools
import math

import jax
import jax.numpy as jnp
from jax import lax
from jax.experimental import pallas as pl
from jax.experimental.pallas import tpu as pltpu

D_MODEL = 1024
HALF = D_MODEL // 2
SG_CHUNK = 128
SG_GROUPS = 4
CV_KERNEL = 31
HEAD_DIM = 64
N_Q_HEADS = 8
N_KV_HEADS = 2
Q_WIDTH = N_Q_HEADS * HEAD_DIM
KV_WIDTH = N_KV_HEADS * HEAD_DIM
WINDOW = 128
ROPE_THETA = 10000.0
SC_KERNEL = 3
N_BRANCH = 4
D_FF = 2816
EPS = 1e-6

OFF_A = 0
OFF_B = OFF_A + 2 * HALF
OFF_Q = OFF_B + 2 * HALF
OFF_K = OFF_Q + Q_WIDTH
OFF_V = OFF_K + KV_WIDTH
OFF_D = OFF_V + KV_WIDTH
OFF_G = OFF_D + 3 * HALF
PROJ_WIDTH = OFF_G + N_BRANCH * D_MODEL

LANES = 128
SUBLANES = 8
BF16_SUBLANES = 16
TILE = 512
FFN_TILE = 1024
CV_HALO = 32
SC_HALO = SUBLANES
CONV_ROWS = 128
MASK_VALUE = -1e30
VMEM_LIMIT_BYTES = 60 * 1024 * 1024

F32 = jnp.float32
BF16 = jnp.bfloat16


def _rmsnorm(x, g):
    return x * lax.rsqrt(jnp.mean(x * x, axis=-1, keepdims=True) + EPS) * g


def _layernorm(x, g, b):
    mu = jnp.mean(x, axis=-1, keepdims=True)
    xc = x - mu
    var = jnp.mean(xc * xc, axis=-1, keepdims=True)
    return xc * lax.rsqrt(var + EPS) * g + b


def _dot(a, b):
    return jnp.dot(a, b, preferred_element_type=F32)


def _dot_nt(a, b):
    return lax.dot_general(a, b, (((1,), (1,)), ((), ())), preferred_element_type=F32)


def _zero_after(t):
    return t[-1:, -LANES:] * 0.0


def _causal_conv_block(buf, w_ref, ksize, halo, r0, l0, after=None):
    rows = CONV_ROWS + SUBLANES
    out = None
    for r in range(min(SUBLANES, ksize)):
        z = None
        for a in range((ksize - 1 - r) // SUBLANES + 1):
            tap = ksize - 1 - (SUBLANES * a + r)
            start = halo + r0 - SUBLANES * (a + 1)
            w_tap = w_ref[tap:tap + 1, l0:l0 + LANES]
            if after is not None:
                w_tap = w_tap + after
            term = w_tap * buf[start:start + rows, l0:l0 + LANES]
            z = term if z is None else z + term
        if r:
            z = pltpu.roll(z, r, 0)
        z = z[SUBLANES:, :]
        out = z if out is None else out + z
    return out


def _mixer_kernel(sink_ref, x_ref, cos_ref, sin_ref, nm_ref, w_in_ref,
                  sg_g_ref, sg_beta_ref, sg_w_ref, sg_bt_ref,
                  cv_w_ref, cv_b_ref, cv_g_ref, cv_beta_ref, sc_w_ref,
                  w_br_ref, w_out_ref, o_ref,
                  cv_buf, sc_buf, k_buf, v_buf, ya_s, yc_s):
    j = pl.program_id(1)

    @pl.when(j == 0)
    def _():
        cv_buf[0:CV_HALO, :] = jnp.zeros((CV_HALO, HALF), F32)
        sc_buf[0:SC_HALO, :] = jnp.zeros((SC_HALO, HALF), F32)
        k_buf[:, 0:WINDOW, :] = jnp.zeros((4, WINDOW, LANES), BF16)
        v_buf[:, 0:WINDOW, :] = jnp.zeros((4, WINDOW, LANES), BF16)

    x = x_ref[...]
    xn_f32 = _rmsnorm(x, nm_ref[...])
    xn = xn_f32.astype(BF16)
    xn_half = (0.5 * xn_f32).astype(BF16)

    def proj(off, width):
        return _dot(xn, w_in_ref[:, off:off + width])

    zb_half = _dot(xn_half, w_in_ref[:, OFF_B:OFF_B + 2 * HALF])
    cv_buf[CV_HALO:CV_HALO + TILE, :] = zb_half[:, :HALF] * (jnp.tanh(zb_half[:, HALF:]) + 1.0)

    cos = cos_ref[...]
    sin = sin_ref[...]
    lane = lax.broadcasted_iota(jnp.int32, (TILE, LANES), 1)
    first_half = (lane % HEAD_DIM) < (HEAD_DIM // 2)
    low_head = lane < HEAD_DIM

    def rotary(t):
        swapped = jnp.where(first_half,
                            pltpu.roll(t, LANES - HEAD_DIM // 2, 1),
                            pltpu.roll(t, HEAD_DIM // 2, 1))
        return t * cos + swapped * sin

    kv = proj(OFF_K, 2 * KV_WIDTH)
    kr = rotary(kv[:, :KV_WIDTH]) * (HEAD_DIM ** -0.5)
    vv = kv[:, KV_WIDTH:]
    for buf, t in ((k_buf, kr), (v_buf, vv)):
        sw = pltpu.roll(t, HEAD_DIM, 1)
        buf[0, WINDOW:WINDOW + TILE, :] = jnp.where(low_head, t, 0.0).astype(BF16)
        buf[1, WINDOW:WINDOW + TILE, :] = jnp.where(low_head, 0.0, sw).astype(BF16)
        buf[2, WINDOW:WINDOW + TILE, :] = jnp.where(low_head, sw, 0.0).astype(BF16)
        buf[3, WINDOW:WINDOW + TILE, :] = jnp.where(low_head, 0.0, t).astype(BF16)

    qz = proj(OFF_Q, Q_WIDTH)
    qi = lax.broadcasted_iota(jnp.int32, (WINDOW, 2 * WINDOW), 0)
    kj = lax.broadcasted_iota(jnp.int32, (WINDOW, 2 * WINDOW), 1)
    band = (kj > qi) & (kj <= qi + WINDOW)
    first_lo = jnp.where(j == 0, WINDOW, 0)
    bias_rest = jnp.where(band, 0.0, MASK_VALUE)
    bias_first = jnp.where(band & (kj >= first_lo), 0.0, MASK_VALUE)
    n_chunks = TILE // WINDOW
    bias2_first = jnp.concatenate([bias_first, bias_first], axis=1)
    bias2_rest = jnp.concatenate([bias_rest, bias_rest], axis=1)
    bias_tile = jnp.concatenate([bias2_first] + [bias2_rest] * (n_chunks - 1), axis=0)
    low_head_t = lax.broadcasted_iota(jnp.int32, (TILE, LANES), 1) < HEAD_DIM

    def attention_group(g, after):
        h = (g * 2) // (N_Q_HEADS // N_KV_HEADS)
        qg = rotary(qz[:, g * LANES:(g + 1) * LANES])
        if after is not None:
            qg = qg + after
        qg = qg.astype(BF16)
        windows = [slice(c * WINDOW, c * WINDOW + 2 * WINDOW) for c in range(n_chunks)]
        s = jnp.concatenate(
            [jnp.concatenate([_dot_nt(qg[c * WINDOW:(c + 1) * WINDOW, :], k_buf[2 * h + side, win, :])
                              for side in range(2)], axis=1)
             for c, win in enumerate(windows)], axis=0) + bias_tile
        probs = []
        inv = []
        for side in range(2):
            s_side = s[:, side * 2 * WINDOW:(side + 1) * 2 * WINDOW]
            sink = sink_ref[2 * g + side]
            m = jnp.maximum(jnp.max(s_side, axis=-1, keepdims=True), sink)
            p = jnp.exp(s_side - m)
            den = jnp.sum(p, axis=-1, keepdims=True) + jnp.exp(sink - m)
            inv.append(1.0 / den)
            probs.append(p.astype(BF16))
        out = jnp.concatenate(
            [_dot(probs[0][c * WINDOW:(c + 1) * WINDOW, :], v_buf[2 * h, win, :])
             + _dot(probs[1][c * WINDOW:(c + 1) * WINDOW, :], v_buf[2 * h + 1, win, :])
             for c, win in enumerate(windows)], axis=0)
        out = out * jnp.where(low_head_t, inv[0], inv[1])
        yc_s[:, g * LANES:(g + 1) * LANES] = out.astype(BF16)

    assert HALF // LANES == N_BRANCH and Q_WIDTH // LANES == N_BRANCH
    conv_cols = []
    gates = []
    after = None
    for n in range(N_BRANCH):
        half_z = _dot(xn_half, w_in_ref[:, OFF_G + n * D_MODEL:OFF_G + (n + 1) * D_MODEL])
        t = jnp.tanh(half_z)
        gates.append(t + 1.0)
        conv_cols.append(jnp.concatenate(
            [_causal_conv_block(cv_buf, cv_w_ref, CV_KERNEL, CV_HALO, r0, n * LANES, after)
             for r0 in range(0, TILE, CONV_ROWS)], axis=0))
        attention_group(n, after)
        after = _zero_after(t)
    for buf in (k_buf, v_buf):
        buf[:, 0:WINDOW, :] = buf[:, TILE:TILE + WINDOW, :]
    conv = jnp.concatenate(conv_cols, axis=1)
    cv_buf[0:CV_HALO, :] = cv_buf[TILE:TILE + CV_HALO, :]

    za_half = _dot(xn_half, w_in_ref[:, OFF_A:OFF_A + 2 * HALF])
    yb_half = _layernorm(conv + (cv_b_ref[...] + after[:, 0:1]),
                         0.5 * cv_g_ref[...], 0.5 * cv_beta_ref[...])
    yb = (yb_half * (jnp.tanh(yb_half) + 1.0)).astype(BF16)

    zd = proj(OFF_D, 3 * HALF)
    ga = za_half * (1.0 + lax.erf(za_half * math.sqrt(2.0)))
    u = ga[:, :HALF]
    vln = _layernorm(ga[:, HALF:], sg_g_ref[...], sg_beta_ref[...]).astype(BF16)

    row = lax.broadcasted_iota(jnp.int32, (SG_CHUNK, SG_CHUNK), 0)
    col = lax.broadcasted_iota(jnp.int32, (SG_CHUNK, SG_CHUNK), 1)
    gw = SG_CHUNK
    for g in range(SG_GROUPS):
        w_g = jnp.where(row >= col, sg_w_ref[g], 0.0).astype(BF16)
        b_g = sg_bt_ref[:, g:g + 1]
        v_wide = jnp.concatenate(
            [vln[r0:r0 + SG_CHUNK, g * gw:(g + 1) * gw] for r0 in range(0, TILE, SG_CHUNK)], axis=1)
        mixed_wide = _dot(w_g, v_wide) + b_g
        mixed = jnp.concatenate(
            [mixed_wide[:, c * gw:(c + 1) * gw] for c in range(TILE // SG_CHUNK)], axis=0)
        ya_s[:, g * gw:(g + 1) * gw] = (u[:, g * gw:(g + 1) * gw] * mixed).astype(BF16)

    sc_buf[SC_HALO:SC_HALO + TILE, :] = zd[:, HALF:2 * HALF] * zd[:, 2 * HALF:]
    conv = jnp.concatenate(
        [jnp.concatenate([_causal_conv_block(sc_buf, sc_w_ref, SC_KERNEL, SC_HALO, r0, l0)
                          for r0 in range(0, TILE, CONV_ROWS)], axis=0)
         for l0 in range(0, HALF, LANES)], axis=1)
    sc_buf[0:SC_HALO, :] = sc_buf[TILE:TILE + SC_HALO, :]
    yd = (zd[:, :HALF] * conv).astype(BF16)

    merged = (gates[1] * _dot(yb, w_br_ref[1]) + gates[2] * _dot(yc_s[...], w_br_ref[2])
              + gates[0] * _dot(ya_s[...], w_br_ref[0]) + gates[3] * _dot(yd, w_br_ref[3]))
    o_ref[...] = x + 0.5 * _dot(merged.astype(BF16), w_out_ref[...])


def _ffn_kernel(*refs, final_norm, n_cast):
    x_ref, nf_ref, w_gu_ref, w_dn_ref, nfinal_ref = refs[:5]
    cast_src = refs[5:5 + n_cast]
    o_ref = refs[5 + n_cast]
    cast_dst = refs[6 + n_cast:]
    x = x_ref[...]
    hn_f32 = _rmsnorm(x, nf_ref[...])
    gate_half = _dot((0.5 * hn_f32).astype(BF16), w_gu_ref[:, :D_FF])
    up = _dot(hn_f32.astype(BF16), w_gu_ref[:, D_FF:])
    h = (gate_half * (jnp.tanh(gate_half) + 1.0) * up).astype(BF16)
    y = x + _dot(h, w_dn_ref[...])
    if final_norm:
        y = _rmsnorm(y, nfinal_ref[...])
    o_ref[...] = y
    for src, dst in zip(cast_src, cast_dst):
        dst[...] = src[...].astype(BF16)


def _resident(shape):
    return pl.BlockSpec(shape, lambda b, j: (0,) * len(shape), pipeline_mode=pl.Buffered(1))


def _cast_rows(total_rows, n_steps):
    for rows in range(BF16_SUBLANES, total_rows + 1, BF16_SUBLANES):
        if total_rows % rows == 0 and rows * n_steps >= total_rows:
            return rows
    raise ValueError((total_rows, n_steps))


def _mixer_call(x, cos_t, sin_t, sinks, nm, w_in, sg_g, sg_beta, sg_w, sg_bt,
                cv_w, cv_b, cv_g, cv_beta, sc_w, w_br, w_out):
    batch, seq, _ = x.shape
    tile_spec = pl.BlockSpec((None, TILE, D_MODEL), lambda b, j: (b, j, 0))
    rope_spec = pl.BlockSpec((TILE, LANES), lambda b, j: (j, 0))
    in_specs = [
        pl.BlockSpec(memory_space=pltpu.SMEM),
        tile_spec, rope_spec, rope_spec,
        _resident((1, D_MODEL)),
        _resident((D_MODEL, PROJ_WIDTH)),
        _resident((1, HALF)), _resident((1, HALF)),
        _resident((SG_GROUPS, SG_CHUNK, SG_CHUNK)), _resident((SG_CHUNK, SG_GROUPS)),
        _resident((CV_KERNEL, HALF)), _resident((1, HALF)), _resident((1, HALF)), _resident((1, HALF)),
        _resident((SC_KERNEL, HALF)),
        _resident((N_BRANCH, HALF, D_MODEL)),
        _resident((D_MODEL, D_MODEL)),
    ]
    scratch = [
        pltpu.VMEM((CV_HALO + TILE, HALF), F32),
        pltpu.VMEM((SC_HALO + TILE, HALF), F32),
        pltpu.VMEM((4, WINDOW + TILE, LANES), BF16),
        pltpu.VMEM((4, WINDOW + TILE, LANES), BF16),
        pltpu.VMEM((TILE, HALF), BF16),
        pltpu.VMEM((TILE, HALF), BF16),
    ]
    return pl.pallas_call(
        _mixer_kernel,
        grid=(batch, seq // TILE),
        in_specs=in_specs,
        out_specs=tile_spec,
        out_shape=jax.ShapeDtypeStruct(x.shape, F32),
        scratch_shapes=scratch,
        compiler_params=pltpu.CompilerParams(
            dimension_semantics=("arbitrary", "arbitrary"),
            vmem_limit_bytes=VMEM_LIMIT_BYTES),
        name="mixer",
    )(sinks, x, cos_t, sin_t, nm, w_in, sg_g, sg_beta, sg_w, sg_bt,
      cv_w, cv_b, cv_g, cv_beta, sc_w, w_br, w_out)


def _ffn_call(x, nf, w_gu, w_dn, nfinal, final_norm, tile, cast_next=(), next_layer=0):
    batch, seq, _ = x.shape
    tiles_per_seq = seq // tile
    n_steps = batch * tiles_per_seq
    tile_spec = pl.BlockSpec((None, tile, D_MODEL), lambda b, j: (b, j, 0))
    src_specs, dst_specs, dst_shapes = [], [], []
    for w in cast_next:
        _, total_rows, cols = w.shape
        rows = _cast_rows(total_rows, n_steps)
        last = total_rows // rows - 1

        def block(b, j, last=last):
            return jnp.minimum(b * tiles_per_seq + j, last)
        src_specs.append(pl.BlockSpec((None, rows, cols), lambda b, j, block=block: (next_layer, block(b, j), 0)))
        dst_specs.append(pl.BlockSpec((rows, cols), lambda b, j, block=block: (block(b, j), 0)))
        dst_shapes.append(jax.ShapeDtypeStruct((total_rows, cols), BF16))
    out = pl.pallas_call(
        functools.partial(_ffn_kernel, final_norm=final_norm, n_cast=len(cast_next)),
        grid=(batch, tiles_per_seq),
        in_specs=[tile_spec, _resident((1, D_MODEL)), _resident((D_MODEL, 2 * D_FF)),
                  _resident((D_FF, D_MODEL)), _resident((1, D_MODEL))] + src_specs,
        out_specs=[tile_spec] + dst_specs,
        out_shape=[jax.ShapeDtypeStruct(x.shape, F32)] + dst_shapes,
        compiler_params=pltpu.CompilerParams(
            dimension_semantics=("arbitrary", "arbitrary"),
            vmem_limit_bytes=VMEM_LIMIT_BYTES),
        name="ffn",
    )(x, nf, w_gu, w_dn, nfinal, *cast_next)
    return out[0], tuple(out[1:])


def _rope_tables(seq):
    pos = jnp.arange(seq, dtype=F32)
    inv_freq = 1.0 / (ROPE_THETA ** (jnp.arange(0, HEAD_DIM, 2, dtype=F32) / HEAD_DIM))
    ang = pos[:, None] * inv_freq[None, :]
    cos = jnp.cos(ang)
    sin = jnp.sin(ang)
    reps = LANES // HEAD_DIM
    cos_t = jnp.tile(jnp.concatenate([cos, cos], axis=-1), (1, reps))
    sin_t = jnp.tile(jnp.concatenate([-sin, sin], axis=-1), (1, reps))
    return cos_t, sin_t


def kernel(x, norm_mix, w_in, sg_ln_g, sg_ln_b, sg_w, sg_b, cv_w, cv_b, cv_ln_g, cv_ln_b,
           attn_sinks, sc_w, w_branch, w_out, norm_ffn, w_gate_up, w_down, norm_final):
    depth = w_in.shape[0]
    seq = x.shape[1]
    assert x.shape[2] == D_MODEL and seq % TILE == 0 and TILE % WINDOW == 0 and seq % FFN_TILE == 0
    cos_t, sin_t = _rope_tables(seq)
    row = lambda a: a.reshape(1, -1)
    stacked = (w_in, w_branch.reshape(depth, N_BRANCH * HALF, D_MODEL), w_out, w_gate_up, w_down)
    weights = tuple(w[0].astype(BF16) for w in stacked)
    for l in range(depth):
        w_in_l, w_br_l, w_out_l, w_gu_l, w_dn_l = weights
        x = _mixer_call(
            x, cos_t, sin_t, attn_sinks[l], row(norm_mix[l]), w_in_l,
            row(sg_ln_g[l]), row(sg_ln_b[l]), sg_w[l], sg_b[l].T,
            cv_w[l], row(cv_b[l]), row(cv_ln_g[l]), row(cv_ln_b[l]), sc_w[l],
            w_br_l.reshape(N_BRANCH, HALF, D_MODEL), w_out_l)
        last = l == depth - 1
        x, weights = _ffn_call(
            x, row(norm_ffn[l]), w_gu_l, w_dn_l, row(norm_final), final_norm=last,
            tile=FFN_TILE if last else TILE, cast_next=() if last else stacked, next_layer=l + 1)
    return x
```

```python
import functools
import math

import jax
import jax.numpy as jnp
from jax import lax
from jax.experimental import pallas as pl
from jax.experimental.pallas import tpu as pltpu

D_MODEL = 1024
HALF = D_MODEL // 2
SG_CHUNK = 128
SG_GROUPS = 4
CV_KERNEL = 31
HEAD_DIM = 64
N_Q_HEADS = 8
N_KV_HEADS = 2
Q_WIDTH = N_Q_HEADS * HEAD_DIM
KV_WIDTH = N_KV_HEADS * HEAD_DIM
WINDOW = 128
ROPE_THETA = 10000.0
SC_KERNEL = 3
N_BRANCH = 4
D_FF = 2816
EPS = 1e-6

OFF_A = 0
OFF_B = OFF_A + 2 * HALF
OFF_Q = OFF_B + 2 * HALF
OFF_K = OFF_Q + Q_WIDTH
OFF_V = OFF_K + KV_WIDTH
OFF_D = OFF_V + KV_WIDTH
OFF_G = OFF_D + 3 * HALF
PROJ_WIDTH = OFF_G + N_BRANCH * D_MODEL

LANES = 128
SUBLANES = 8
BF16_SUBLANES = 16
TILE = 512
FFN_TILE = 1024
CV_HALO = 32
SC_HALO = SUBLANES
CONV_ROWS = 128
MASK_VALUE = -1e30
N_MIXER_INPUTS = 17
VMEM_LIMIT_BYTES = 60 * 1024 * 1024
MIXER_VMEM_LIMIT_BYTES = 62 * 1024 * 1024

F32 = jnp.float32
BF16 = jnp.bfloat16


def _rmsnorm(x, g):
    return x * lax.rsqrt(jnp.mean(x * x, axis=-1, keepdims=True) + EPS) * g


def _layernorm(x, g, b):
    mu = jnp.mean(x, axis=-1, keepdims=True)
    xc = x - mu
    var = jnp.mean(xc * xc, axis=-1, keepdims=True)
    return xc * lax.rsqrt(var + EPS) * g + b


def _dot(a, b):
    return jnp.dot(a, b, preferred_element_type=F32)


def _dot_nt(a, b):
    return lax.dot_general(a, b, (((1,), (1,)), ((), ())), preferred_element_type=F32)


def _zero_after(t):
    return t[-1:, -LANES:] * 0.0


def _causal_conv_block(buf, w_ref, ksize, halo, r0, l0, after=None):
    rows = CONV_ROWS + SUBLANES
    out = None
    for r in range(min(SUBLANES, ksize)):
        z = None
        for a in range((ksize - 1 - r) // SUBLANES + 1):
            tap = ksize - 1 - (SUBLANES * a + r)
            start = halo + r0 - SUBLANES * (a + 1)
            w_tap = w_ref[tap:tap + 1, l0:l0 + LANES]
            if after is not None:
                w_tap = w_tap + after
            term = w_tap * buf[start:start + rows, l0:l0 + LANES]
            z = term if z is None else z + term
        if r:
            z = pltpu.roll(z, r, 0)
        z = z[SUBLANES:, :]
        out = z if out is None else out + z
    return out


def _mixer_kernel(*refs, n_cast):
    (sink_ref, x_ref, cos_ref, sin_ref, nm_ref, w_in_ref,
     sg_g_ref, sg_beta_ref, sg_w_ref, sg_bt_ref,
     cv_w_ref, cv_b_ref, cv_g_ref, cv_beta_ref, sc_w_ref,
     w_br_ref, w_out_ref) = refs[:N_MIXER_INPUTS]
    cast_src = refs[N_MIXER_INPUTS:N_MIXER_INPUTS + n_cast]
    o_ref = refs[N_MIXER_INPUTS + n_cast]
    cast_dst = refs[N_MIXER_INPUTS + n_cast + 1:N_MIXER_INPUTS + 2 * n_cast + 1]
    cv_buf, sc_buf, k_buf, v_buf, ya_s, yc_s = refs[N_MIXER_INPUTS + 2 * n_cast + 1:]
    j = pl.program_id(1)

    @pl.when(j == 0)
    def _():
        cv_buf[0:CV_HALO, :] = jnp.zeros((CV_HALO, HALF), F32)
        sc_buf[0:SC_HALO, :] = jnp.zeros((SC_HALO, HALF), F32)
        k_buf[:, 0:WINDOW, :] = jnp.zeros((4, WINDOW, LANES), BF16)
        v_buf[:, 0:WINDOW, :] = jnp.zeros((4, WINDOW, LANES), BF16)

    x = x_ref[...]
    xn_f32 = _rmsnorm(x, nm_ref[...])
    xn = xn_f32.astype(BF16)
    xn_half = (0.5 * xn_f32).astype(BF16)

    def proj(off, width):
        return _dot(xn, w_in_ref[:, off:off + width])

    zb_half = _dot(xn_half, w_in_ref[:, OFF_B:OFF_B + 2 * HALF])
    cv_buf[CV_HALO:CV_HALO + TILE, :] = zb_half[:, :HALF] * (jnp.tanh(zb_half[:, HALF:]) + 1.0)

    cos = cos_ref[...]
    sin = sin_ref[...]
    lane = lax.broadcasted_iota(jnp.int32, (TILE, LANES), 1)
    first_half = (lane % HEAD_DIM) < (HEAD_DIM // 2)
    low_head = lane < HEAD_DIM

    def rotary(t):
        swapped = jnp.where(first_half,
                            pltpu.roll(t, LANES - HEAD_DIM // 2, 1),
                            pltpu.roll(t, HEAD_DIM // 2, 1))
        return t * cos + swapped * sin

    kv = proj(OFF_K, 2 * KV_WIDTH)
    kr = rotary(kv[:, :KV_WIDTH]) * (HEAD_DIM ** -0.5)
    vv = kv[:, KV_WIDTH:]
    for buf, t in ((k_buf, kr), (v_buf, vv)):
        sw = pltpu.roll(t, HEAD_DIM, 1)
        buf[0, WINDOW:WINDOW + TILE, :] = jnp.where(low_head, t, 0.0).astype(BF16)
        buf[1, WINDOW:WINDOW + TILE, :] = jnp.where(low_head, 0.0, sw).astype(BF16)
        buf[2, WINDOW:WINDOW + TILE, :] = jnp.where(low_head, sw, 0.0).astype(BF16)
        buf[3, WINDOW:WINDOW + TILE, :] = jnp.where(low_head, 0.0, t).astype(BF16)

    qz = proj(OFF_Q, Q_WIDTH)
    qi = lax.broadcasted_iota(jnp.int32, (WINDOW, 2 * WINDOW), 0)
    kj = lax.broadcasted_iota(jnp.int32, (WINDOW, 2 * WINDOW), 1)
    band = (kj > qi) & (kj <= qi + WINDOW)
    first_lo = jnp.where(j == 0, WINDOW, 0)
    bias_rest = jnp.where(band, 0.0, MASK_VALUE)
    bias_first = jnp.where(band & (kj >= first_lo), 0.0, MASK_VALUE)
    n_chunks = TILE // WINDOW
    bias2_first = jnp.concatenate([bias_first, bias_first], axis=1)
    bias2_rest = jnp.concatenate([bias_rest, bias_rest], axis=1)
    bias_tile = jnp.concatenate([bias2_first] + [bias2_rest] * (n_chunks - 1), axis=0)
    low_head_t = lax.broadcasted_iota(jnp.int32, (TILE, LANES), 1) < HEAD_DIM

    def attention_group(g, after):
        h = (g * 2) // (N_Q_HEADS // N_KV_HEADS)
        qg = rotary(qz[:, g * LANES:(g + 1) * LANES])
        if after is not None:
            qg = qg + after
        qg = qg.astype(BF16)
        windows = [slice(c * WINDOW, c * WINDOW + 2 * WINDOW) for c in range(n_chunks)]
        s = jnp.concatenate(
            [jnp.concatenate([_dot_nt(qg[c * WINDOW:(c + 1) * WINDOW, :], k_buf[2 * h + side, win, :])
                              for side in range(2)], axis=1)
             for c, win in enumerate(windows)], axis=0) + bias_tile
        probs = []
        inv = []
        for side in range(2):
            s_side = s[:, side * 2 * WINDOW:(side + 1) * 2 * WINDOW]
            sink = sink_ref[2 * g + side]
            m = jnp.maximum(jnp.max(s_side, axis=-1, keepdims=True), sink)
            p = jnp.exp(s_side - m)
            den = jnp.sum(p, axis=-1, keepdims=True) + jnp.exp(sink - m)
            inv.append(1.0 / den)
            probs.append(p.astype(BF16))
        out = jnp.concatenate(
            [_dot(probs[0][c * WINDOW:(c + 1) * WINDOW, :], v_buf[2 * h, win, :])
             + _dot(probs[1][c * WINDOW:(c + 1) * WINDOW, :], v_buf[2 * h + 1, win, :])
             for c, win in enumerate(windows)], axis=0)
        out = out * jnp.where(low_head_t, inv[0], inv[1])
        yc_s[:, g * LANES:(g + 1) * LANES] = out.astype(BF16)

    assert HALF // LANES == N_BRANCH and Q_WIDTH // LANES == N_BRANCH
    conv_cols = []
    gates = []
    after = None
    for n in range(N_BRANCH):
        half_z = _dot(xn_half, w_in_ref[:, OFF_G + n * D_MODEL:OFF_G + (n + 1) * D_MODEL])
        t = jnp.tanh(half_z)
        gates.append(t + 1.0)
        conv_cols.append(jnp.concatenate(
            [_causal_conv_block(cv_buf, cv_w_ref, CV_KERNEL, CV_HALO, r0, n * LANES, after)
             for r0 in range(0, TILE, CONV_ROWS)], axis=0))
        attention_group(n, after)
        after = _zero_after(t)
    for buf in (k_buf, v_buf):
        buf[:, 0:WINDOW, :] = buf[:, TILE:TILE + WINDOW, :]
    conv = jnp.concatenate(conv_cols, axis=1)
    cv_buf[0:CV_HALO, :] = cv_buf[TILE:TILE + CV_HALO, :]

    za_half = _dot(xn_half, w_in_ref[:, OFF_A:OFF_A + 2 * HALF])
    yb_half = _layernorm(conv + (cv_b_ref[...] + after[:, 0:1]),
                         0.5 * cv_g_ref[...], 0.5 * cv_beta_ref[...])
    yb = (yb_half * (jnp.tanh(yb_half) + 1.0)).astype(BF16)

    zd = proj(OFF_D, 3 * HALF)
    ga = za_half * (1.0 + lax.erf(za_half * math.sqrt(2.0)))
    u = ga[:, :HALF]
    vln = _layernorm(ga[:, HALF:], sg_g_ref[...], sg_beta_ref[...]).astype(BF16)

    row = lax.broadcasted_iota(jnp.int32, (SG_CHUNK, SG_CHUNK), 0)
    col = lax.broadcasted_iota(jnp.int32, (SG_CHUNK, SG_CHUNK), 1)
    gw = SG_CHUNK
    for g in range(SG_GROUPS):
        w_g = jnp.where(row >= col, sg_w_ref[g], 0.0).astype(BF16)
        b_g = sg_bt_ref[:, g:g + 1]
        v_wide = jnp.concatenate(
            [vln[r0:r0 + SG_CHUNK, g * gw:(g + 1) * gw] for r0 in range(0, TILE, SG_CHUNK)], axis=1)
        mixed_wide = _dot(w_g, v_wide) + b_g
        mixed = jnp.concatenate(
            [mixed_wide[:, c * gw:(c + 1) * gw] for c in range(TILE // SG_CHUNK)], axis=0)
        ya_s[:, g * gw:(g + 1) * gw] = (u[:, g * gw:(g + 1) * gw] * mixed).astype(BF16)

    sc_buf[SC_HALO:SC_HALO + TILE, :] = zd[:, HALF:2 * HALF] * zd[:, 2 * HALF:]
    conv = jnp.concatenate(
        [jnp.concatenate([_causal_conv_block(sc_buf, sc_w_ref, SC_KERNEL, SC_HALO, r0, l0)
                          for r0 in range(0, TILE, CONV_ROWS)], axis=0)
         for l0 in range(0, HALF, LANES)], axis=1)
    sc_buf[0:SC_HALO, :] = sc_buf[TILE:TILE + SC_HALO, :]
    yd = (zd[:, :HALF] * conv).astype(BF16)

    merged = (gates[1] * _dot(yb, w_br_ref[1]) + gates[2] * _dot(yc_s[...], w_br_ref[2])
              + gates[0] * _dot(ya_s[...], w_br_ref[0]) + gates[3] * _dot(yd, w_br_ref[3]))
    o_ref[...] = x + 0.5 * _dot(merged.astype(BF16), w_out_ref[...])
    for src, dst in zip(cast_src, cast_dst):
        dst[...] = src[...].astype(BF16)


def _ffn_kernel(*refs, final_norm, n_cast):
    x_ref, nf_ref, w_gu_ref, w_dn_ref, nfinal_ref = refs[:5]
    cast_src = refs[5:5 + n_cast]
    o_ref = refs[5 + n_cast]
    cast_dst = refs[6 + n_cast:]
    x = x_ref[...]
    hn_f32 = _rmsnorm(x, nf_ref[...])
    gate_half = _dot((0.5 * hn_f32).astype(BF16), w_gu_ref[:, :D_FF])
    up = _dot(hn_f32.astype(BF16), w_gu_ref[:, D_FF:])
    h = (gate_half * (jnp.tanh(gate_half) + 1.0) * up).astype(BF16)
    y = x + _dot(h, w_dn_ref[...])
    if final_norm:
        y = _rmsnorm(y, nfinal_ref[...])
    o_ref[...] = y
    for src, dst in zip(cast_src, cast_dst):
        dst[...] = src[...].astype(BF16)


def _resident(shape):
    return pl.BlockSpec(shape, lambda b, j: (0,) * len(shape), pipeline_mode=pl.Buffered(1))


def _cast_rows(total_rows, n_steps):
    for rows in range(BF16_SUBLANES, total_rows + 1, BF16_SUBLANES):
        if total_rows % rows == 0 and rows * n_steps >= total_rows:
            return rows
    raise ValueError((total_rows, n_steps))


def _cast_specs(cast_src, layer, tiles_per_seq, n_steps):
    src_specs, dst_specs, dst_shapes = [], [], []
    for w in cast_src:
        _, total_rows, cols = w.shape
        rows = _cast_rows(total_rows, n_steps)
        last = total_rows // rows - 1

        def block(b, j, last=last):
            return jnp.minimum(b * tiles_per_seq + j, last)
        src_specs.append(pl.BlockSpec((None, rows, cols), lambda b, j, block=block: (layer, block(b, j), 0)))
        dst_specs.append(pl.BlockSpec((rows, cols), lambda b, j, block=block: (block(b, j), 0)))
        dst_shapes.append(jax.ShapeDtypeStruct((total_rows, cols), BF16))
    return src_specs, dst_specs, dst_shapes


def _mixer_call(x, cos_t, sin_t, sinks, nm, w_in, sg_g, sg_beta, sg_w, sg_bt,
                cv_w, cv_b, cv_g, cv_beta, sc_w, w_br, w_out, cast_src=(), cast_layer=0):
    batch, seq, _ = x.shape
    tiles_per_seq = seq // TILE
    src_specs, dst_specs, dst_shapes = _cast_specs(cast_src, cast_layer, tiles_per_seq, batch * tiles_per_seq)
    tile_spec = pl.BlockSpec((None, TILE, D_MODEL), lambda b, j: (b, j, 0))
    rope_spec = pl.BlockSpec((TILE, LANES), lambda b, j: (j, 0))
    in_specs = [
        pl.BlockSpec(memory_space=pltpu.SMEM),
        tile_spec, rope_spec, rope_spec,
        _resident((1, D_MODEL)),
        _resident((D_MODEL, PROJ_WIDTH)),
        _resident((1, HALF)), _resident((1, HALF)),
        _resident((SG_GROUPS, SG_CHUNK, SG_CHUNK)), _resident((SG_CHUNK, SG_GROUPS)),
        _resident((CV_KERNEL, HALF)), _resident((1, HALF)), _resident((1, HALF)), _resident((1, HALF)),
        _resident((SC_KERNEL, HALF)),
        _resident((N_BRANCH, HALF, D_MODEL)),
        _resident((D_MODEL, D_MODEL)),
    ]
    scratch = [
        pltpu.VMEM((CV_HALO + TILE, HALF), F32),
        pltpu.VMEM((SC_HALO + TILE, HALF), F32),
        pltpu.VMEM((4, WINDOW + TILE, LANES), BF16),
        pltpu.VMEM((4, WINDOW + TILE, LANES), BF16),
        pltpu.VMEM((TILE, HALF), BF16),
        pltpu.VMEM((TILE, HALF), BF16),
    ]
    assert len(in_specs) == N_MIXER_INPUTS
    out = pl.pallas_call(
        functools.partial(_mixer_kernel, n_cast=len(cast_src)),
        grid=(batch, tiles_per_seq),
        in_specs=in_specs + src_specs,
        out_specs=[tile_spec] + dst_specs,
        out_shape=[jax.ShapeDtypeStruct(x.shape, F32)] + dst_shapes,
        scratch_shapes=scratch,
        compiler_params=pltpu.CompilerParams(
            dimension_semantics=("arbitrary", "arbitrary"),
            vmem_limit_bytes=MIXER_VMEM_LIMIT_BYTES),
        name="mixer",
    )(sinks, x, cos_t, sin_t, nm, w_in, sg_g, sg_beta, sg_w, sg_bt,
      cv_w, cv_b, cv_g, cv_beta, sc_w, w_br, w_out, *cast_src)
    return out[0], tuple(out[1:])


def _ffn_call(x, nf, w_gu, w_dn, nfinal, final_norm, tile, cast_next=(), next_layer=0):
    batch, seq, _ = x.shape
    tiles_per_seq = seq // tile
    n_steps = batch * tiles_per_seq
    tile_spec = pl.BlockSpec((None, tile, D_MODEL), lambda b, j: (b, j, 0))
    src_specs, dst_specs, dst_shapes = _cast_specs(cast_next, next_layer, tiles_per_seq, n_steps)
    out = pl.pallas_call(
        functools.partial(_ffn_kernel, final_norm=final_norm, n_cast=len(cast_next)),
        grid=(batch, tiles_per_seq),
        in_specs=[tile_spec, _resident((1, D_MODEL)), _resident((D_MODEL, 2 * D_FF)),
                  _resident((D_FF, D_MODEL)), _resident((1, D_MODEL))] + src_specs,
        out_specs=[tile_spec] + dst_specs,
        out_shape=[jax.ShapeDtypeStruct(x.shape, F32)] + dst_shapes,
        compiler_params=pltpu.CompilerParams(
            dimension_semantics=("arbitrary", "arbitrary"),
            vmem_limit_bytes=VMEM_LIMIT_BYTES),
        name="ffn",
    )(x, nf, w_gu, w_dn, nfinal, *cast_next)
    return out[0], tuple(out[1:])


def _rope_tables(seq):
    pos = jnp.arange(seq, dtype=F32)
    inv_freq = 1.0 / (ROPE_THETA ** (jnp.arange(0, HEAD_DIM, 2, dtype=F32) / HEAD_DIM))
    ang = pos[:, None] * inv_freq[None, :]
    cos = jnp.cos(ang)
    sin = jnp.sin(ang)
    reps = LANES // HEAD_DIM
    cos_t = jnp.tile(jnp.concatenate([cos, cos], axis=-1), (1, reps))
    sin_t = jnp.tile(jnp.concatenate([-sin, sin], axis=-1), (1, reps))
    return cos_t, sin_t


def kernel(x, norm_mix, w_in, sg_ln_g, sg_ln_b, sg_w, sg_b, cv_w, cv_b, cv_ln_g, cv_ln_b,
           attn_sinks, sc_w, w_branch, w_out, norm_ffn, w_gate_up, w_down, norm_final):
    depth = w_in.shape[0]
    seq = x.shape[1]
    assert x.shape[2] == D_MODEL and seq % TILE == 0 and TILE % WINDOW == 0 and seq % FFN_TILE == 0
    cos_t, sin_t = _rope_tables(seq)
    row = lambda a: a.reshape(1, -1)
    mixer_stacked = (w_in, w_branch.reshape(depth, N_BRANCH * HALF, D_MODEL), w_out)
    ffn_stacked = (w_gate_up, w_down)
    stacked = mixer_stacked + ffn_stacked
    weights = tuple(w[0].astype(BF16) for w in mixer_stacked)
    for l in range(depth):
        w_in_l, w_br_l, w_out_l = weights[:3]
        x, ffn_weights = _mixer_call(
            x, cos_t, sin_t, attn_sinks[l], row(norm_mix[l]), w_in_l,
            row(sg_ln_g[l]), row(sg_ln_b[l]), sg_w[l], sg_b[l].T,
            cv_w[l], row(cv_b[l]), row(cv_ln_g[l]), row(cv_ln_b[l]), sc_w[l],
            w_br_l.reshape(N_BRANCH, HALF, D_MODEL), w_out_l,
            cast_src=ffn_stacked if l == 0 else (), cast_layer=l)
        w_gu_l, w_dn_l = ffn_weights if l == 0 else weights[3:]
        last = l == depth - 1
        x, weights = _ffn_call(
            x, row(norm_ffn[l]), w_gu_l, w_dn_l, row(norm_final), final_norm=last,
            tile=FFN_TILE if last else TILE, cast_next=() if last else stacked, next_layer=l + 1)
    return x
```

```python
import functools
import math

import jax
import jax.numpy as jnp
from jax import lax
from jax.experimental import pallas as pl
from jax.experimental.pallas import tpu as pltpu

D_MODEL = 1024
HALF = D_MODEL // 2
SG_CHUNK = 128
SG_GROUPS = 4
CV_KERNEL = 31
HEAD_DIM = 64
N_Q_HEADS = 8
N_KV_HEADS = 2
Q_WIDTH = N_Q_HEADS * HEAD_DIM
KV_WIDTH = N_KV_HEADS * HEAD_DIM
WINDOW = 128
ROPE_THETA = 10000.0
SC_KERNEL = 3
N_BRANCH = 4
D_FF = 2816
EPS = 1e-6

OFF_A = 0
OFF_B = OFF_A + 2 * HALF
OFF_Q = OFF_B + 2 * HALF
OFF_K = OFF_Q + Q_WIDTH
OFF_V = OFF_K + KV_WIDTH
OFF_D = OFF_V + KV_WIDTH
OFF_G = OFF_D + 3 * HALF
PROJ_WIDTH = OFF_G + N_BRANCH * D_MODEL

LANES = 128
SUBLANES = 8
BF16_SUBLANES = 16
TILE = 512
FFN_TILE = 1024
CV_HALO = 32
SC_HALO = SUBLANES
CONV_ROWS = 128
MASK_VALUE = -1e30
N_MIXER_INPUTS = 17
VMEM_LIMIT_BYTES = 60 * 1024 * 1024
MIXER_VMEM_LIMIT_BYTES = 62 * 1024 * 1024

F32 = jnp.float32
BF16 = jnp.bfloat16


def _rmsnorm(x, g):
    return x * lax.rsqrt(jnp.mean(x * x, axis=-1, keepdims=True) + EPS) * g


def _layernorm(x, g, b):
    mu = jnp.mean(x, axis=-1, keepdims=True)
    xc = x - mu
    var = jnp.mean(xc * xc, axis=-1, keepdims=True)
    return xc * lax.rsqrt(var + EPS) * g + b


def _dot(a, b):
    return jnp.dot(a, b, preferred_element_type=F32)


def _dot_nt(a, b):
    return lax.dot_general(a, b, (((1,), (1,)), ((), ())), preferred_element_type=F32)


def _zero_after(t):
    return t[-1:, -LANES:] * 0.0


def _causal_conv_block(buf, w_ref, ksize, halo, r0, l0, after=None):
    rows = CONV_ROWS + SUBLANES
    out = None
    for r in range(min(SUBLANES, ksize)):
        z = None
        for a in range((ksize - 1 - r) // SUBLANES + 1):
            tap = ksize - 1 - (SUBLANES * a + r)
            start = halo + r0 - SUBLANES * (a + 1)
            w_tap = w_ref[tap:tap + 1, l0:l0 + LANES]
            if after is not None:
                w_tap = w_tap + after
            term = w_tap * buf[start:start + rows, l0:l0 + LANES]
            z = term if z is None else z + term
        if r:
            z = pltpu.roll(z, r, 0)
        z = z[SUBLANES:, :]
        out = z if out is None else out + z
    return out


def _mixer_kernel(*refs, n_cast):
    (sink_ref, x_ref, cos_ref, sin_ref, nm_ref, w_in_ref,
     sg_g_ref, sg_beta_ref, sg_w_ref, sg_bt_ref,
     cv_w_ref, cv_b_ref, cv_g_ref, cv_beta_ref, sc_w_ref,
     w_br_ref, w_out_ref) = refs[:N_MIXER_INPUTS]
    cast_src = refs[N_MIXER_INPUTS:N_MIXER_INPUTS + n_cast]
    o_ref = refs[N_MIXER_INPUTS + n_cast]
    cast_dst = refs[N_MIXER_INPUTS + n_cast + 1:N_MIXER_INPUTS + 2 * n_cast + 1]
    cv_buf, sc_buf, k_buf, v_buf, ya_s, yc_s = refs[N_MIXER_INPUTS + 2 * n_cast + 1:]
    j = pl.program_id(1)

    @pl.when(j == 0)
    def _():
        cv_buf[0:CV_HALO, :] = jnp.zeros((CV_HALO, HALF), F32)
        sc_buf[0:SC_HALO, :] = jnp.zeros((SC_HALO, HALF), F32)
        k_buf[:, 0:WINDOW, :] = jnp.zeros((4, WINDOW, LANES), BF16)
        v_buf[:, 0:WINDOW, :] = jnp.zeros((4, WINDOW, LANES), BF16)

    x = x_ref[...]
    xn_f32 = _rmsnorm(x, nm_ref[...])
    xn = xn_f32.astype(BF16)
    xn_half = (0.5 * xn_f32).astype(BF16)

    def proj(off, width):
        return _dot(xn, w_in_ref[:, off:off + width])

    zb_half = _dot(xn_half, w_in_ref[:, OFF_B:OFF_B + 2 * HALF])
    cv_buf[CV_HALO:CV_HALO + TILE, :] = zb_half[:, :HALF] * (jnp.tanh(zb_half[:, HALF:]) + 1.0)

    cos = cos_ref[...]
    sin = sin_ref[...]
    lane = lax.broadcasted_iota(jnp.int32, (TILE, LANES), 1)
    first_half = (lane % HEAD_DIM) < (HEAD_DIM // 2)
    low_head = lane < HEAD_DIM

    def rotary(t):
        swapped = jnp.where(first_half,
                            pltpu.roll(t, LANES - HEAD_DIM // 2, 1),
                            pltpu.roll(t, HEAD_DIM // 2, 1))
        return t * cos + swapped * sin

    kv = proj(OFF_K, 2 * KV_WIDTH)
    kr = rotary(kv[:, :KV_WIDTH]) * (HEAD_DIM ** -0.5)
    vv = kv[:, KV_WIDTH:]
    for buf, t in ((k_buf, kr), (v_buf, vv)):
        sw = pltpu.roll(t, HEAD_DIM, 1)
        buf[0, WINDOW:WINDOW + TILE, :] = jnp.where(low_head, t, 0.0).astype(BF16)
        buf[1, WINDOW:WINDOW + TILE, :] = jnp.where(low_head, 0.0, sw).astype(BF16)
        buf[2, WINDOW:WINDOW + TILE, :] = jnp.where(low_head, sw, 0.0).astype(BF16)
        buf[3, WINDOW:WINDOW + TILE, :] = jnp.where(low_head, 0.0, t).astype(BF16)

    qz = proj(OFF_Q, Q_WIDTH)
    qi = lax.broadcasted_iota(jnp.int32, (WINDOW, 2 * WINDOW), 0)
    kj = lax.broadcasted_iota(jnp.int32, (WINDOW, 2 * WINDOW), 1)
    band = (kj > qi) & (kj <= qi + WINDOW)
    first_lo = jnp.where(j == 0, WINDOW, 0)
    bias_rest = jnp.where(band, 0.0, MASK_VALUE)
    bias_first = jnp.where(band & (kj >= first_lo), 0.0, MASK_VALUE)
    n_chunks = TILE // WINDOW
    bias2_first = jnp.concatenate([bias_first, bias_first], axis=1)
    bias2_rest = jnp.concatenate([bias_rest, bias_rest], axis=1)
    bias_tile = jnp.concatenate([bias2_first] + [bias2_rest] * (n_chunks - 1), axis=0)
    low_head_t = lax.broadcasted_iota(jnp.int32, (TILE, LANES), 1) < HEAD_DIM

    def attention_group(g, after):
        h = (g * 2) // (N_Q_HEADS // N_KV_HEADS)
        qg = rotary(qz[:, g * LANES:(g + 1) * LANES])
        if after is not None:
            qg = qg + after
        qg = qg.astype(BF16)
        windows = [slice(c * WINDOW, c * WINDOW + 2 * WINDOW) for c in range(n_chunks)]
        s = jnp.concatenate(
            [jnp.concatenate([_dot_nt(qg[c * WINDOW:(c + 1) * WINDOW, :], k_buf[2 * h + side, win, :])
                              for side in range(2)], axis=1)
             for c, win in enumerate(windows)], axis=0) + bias_tile
        probs = []
        inv = []
        for side in range(2):
            s_side = s[:, side * 2 * WINDOW:(side + 1) * 2 * WINDOW]
            sink = sink_ref[2 * g + side]
            m = jnp.maximum(jnp.max(s_side, axis=-1, keepdims=True), sink)
            p = jnp.exp(s_side - m)
            den = jnp.sum(p, axis=-1, keepdims=True) + jnp.exp(sink - m)
            inv.append(1.0 / den)
            probs.append(p.astype(BF16))
        out = jnp.concatenate(
            [_dot(probs[0][c * WINDOW:(c + 1) * WINDOW, :], v_buf[2 * h, win, :])
             + _dot(probs[1][c * WINDOW:(c + 1) * WINDOW, :], v_buf[2 * h + 1, win, :])
             for c, win in enumerate(windows)], axis=0)
        out = out * jnp.where(low_head_t, inv[0], inv[1])
        yc_s[:, g * LANES:(g + 1) * LANES] = out.astype(BF16)

    assert HALF // LANES == N_BRANCH and Q_WIDTH // LANES == N_BRANCH
    conv_cols = []
    gates = []
    after = None
    for n in range(N_BRANCH):
        half_z = _dot(xn_half, w_in_ref[:, OFF_G + n * D_MODEL:OFF_G + (n + 1) * D_MODEL])
        t = jnp.tanh(half_z)
        gates.append(t + 1.0)
        conv_cols.append(jnp.concatenate(
            [_causal_conv_block(cv_buf, cv_w_ref, CV_KERNEL, CV_HALO, r0, n * LANES, after)
             for r0 in range(0, TILE, CONV_ROWS)], axis=0))
        attention_group(n, after)
        after = _zero_after(t)
    for buf in (k_buf, v_buf):
        buf[:, 0:WINDOW, :] = buf[:, TILE:TILE + WINDOW, :]
    conv = jnp.concatenate(conv_cols, axis=1)
    cv_buf[0:CV_HALO, :] = cv_buf[TILE:TILE + CV_HALO, :]

    za_half = _dot(xn_half, w_in_ref[:, OFF_A:OFF_A + 2 * HALF])
    yb_half = _layernorm(conv + (cv_b_ref[...] + after[:, 0:1]),
                         0.5 * cv_g_ref[...], 0.5 * cv_beta_ref[...])
    yb = (yb_half * (jnp.tanh(yb_half) + 1.0)).astype(BF16)

    zd = proj(OFF_D, 3 * HALF)
    ga = za_half * (1.0 + lax.erf(za_half * math.sqrt(2.0)))
    u = ga[:, :HALF]
    vln = _layernorm(ga[:, HALF:], sg_g_ref[...], sg_beta_ref[...]).astype(BF16)

    row = lax.broadcasted_iota(jnp.int32, (SG_CHUNK, SG_CHUNK), 0)
    col = lax.broadcasted_iota(jnp.int32, (SG_CHUNK, SG_CHUNK), 1)
    gw = SG_CHUNK
    for g in range(SG_GROUPS):
        w_g = jnp.where(row >= col, sg_w_ref[g], 0.0).astype(BF16)
        b_g = sg_bt_ref[:, g:g + 1]
        v_wide = jnp.concatenate(
            [vln[r0:r0 + SG_CHUNK, g * gw:(g + 1) * gw] for r0 in range(0, TILE, SG_CHUNK)], axis=1)
        mixed_wide = _dot(w_g, v_wide) + b_g
        mixed = jnp.concatenate(
            [mixed_wide[:, c * gw:(c + 1) * gw] for c in range(TILE // SG_CHUNK)], axis=0)
        ya_s[:, g * gw:(g + 1) * gw] = (u[:, g * gw:(g + 1) * gw] * mixed).astype(BF16)

    sc_buf[SC_HALO:SC_HALO + TILE, :] = zd[:, HALF:2 * HALF] * zd[:, 2 * HALF:]
    conv = jnp.concatenate(
        [jnp.concatenate([_causal_conv_block(sc_buf, sc_w_ref, SC_KERNEL, SC_HALO, r0, l0)
                          for r0 in range(0, TILE, CONV_ROWS)], axis=0)
         for l0 in range(0, HALF, LANES)], axis=1)
    sc_buf[0:SC_HALO, :] = sc_buf[TILE:TILE + SC_HALO, :]
    yd = (zd[:, :HALF] * conv).astype(BF16)

    merged = (gates[1] * _dot(yb, w_br_ref[1]) + gates[2] * _dot(yc_s[...], w_br_ref[2])
              + gates[0] * _dot(ya_s[...], w_br_ref[0]) + gates[3] * _dot(yd, w_br_ref[3]))
    o_ref[...] = x + 0.5 * _dot(merged.astype(BF16), w_out_ref[...])
    for src, dst in zip(cast_src, cast_dst):
        dst[...] = src[...].astype(BF16)


def _ffn_kernel(*refs, final_norm, n_cast):
    x_ref, nf_ref, w_gu_ref, w_dn_ref, nfinal_ref = refs[:5]
    cast_src = refs[5:5 + n_cast]
    o_ref = refs[5 + n_cast]
    cast_dst = refs[6 + n_cast:]
    x = x_ref[...]
    hn_f32 = _rmsnorm(x, nf_ref[...])
    gate_half = _dot((0.5 * hn_f32).astype(BF16), w_gu_ref[:, :D_FF])
    up = _dot(hn_f32.astype(BF16), w_gu_ref[:, D_FF:])
    h = (gate_half * (jnp.tanh(gate_half) + 1.0) * up).astype(BF16)
    y = x + _dot(h, w_dn_ref[...])
    if final_norm:
        y = _rmsnorm(y, nfinal_ref[...])
    o_ref[...] = y
    for src, dst in zip(cast_src, cast_dst):
        dst[...] = src[...].astype(BF16)


def _resident(shape):
    return pl.BlockSpec(shape, lambda b, j: (0,) * len(shape), pipeline_mode=pl.Buffered(1))


def _cast_rows(total_rows, n_steps):
    for rows in range(BF16_SUBLANES, total_rows + 1, BF16_SUBLANES):
        if total_rows % rows == 0 and rows * n_steps >= total_rows:
            return rows
    raise ValueError((total_rows, n_steps))


def _cast_specs(cast_src, layer, tiles_per_seq, n_steps):
    src_specs, dst_specs, dst_shapes = [], [], []
    for w in cast_src:
        _, total_rows, cols = w.shape
        rows = _cast_rows(total_rows, n_steps)
        last = total_rows // rows - 1

        def block(b, j, last=last):
            return jnp.minimum(b * tiles_per_seq + j, last)
        src_specs.append(pl.BlockSpec((None, rows, cols), lambda b, j, block=block: (layer, block(b, j), 0)))
        dst_specs.append(pl.BlockSpec((rows, cols), lambda b, j, block=block: (block(b, j), 0)))
        dst_shapes.append(jax.ShapeDtypeStruct((total_rows, cols), BF16))
    return src_specs, dst_specs, dst_shapes


def _mixer_call(x, cos_t, sin_t, sinks, nm, w_in, sg_g, sg_beta, sg_w, sg_bt,
                cv_w, cv_b, cv_g, cv_beta, sc_w, w_br, w_out, cast_src=(), cast_layer=0):
    batch, seq, _ = x.shape
    tiles_per_seq = seq // TILE
    src_specs, dst_specs, dst_shapes = _cast_specs(cast_src, cast_layer, tiles_per_seq, batch * tiles_per_seq)
    tile_spec = pl.BlockSpec((None, TILE, D_MODEL), lambda b, j: (b, j, 0))
    rope_spec = pl.BlockSpec((TILE, LANES), lambda b, j: (j, 0))
    in_specs = [
        pl.BlockSpec(memory_space=pltpu.SMEM),
        tile_spec, rope_spec, rope_spec,
        _resident((1, D_MODEL)),
        _resident((D_MODEL, PROJ_WIDTH)),
        _resident((1, HALF)), _resident((1, HALF)),
        _resident((SG_GROUPS, SG_CHUNK, SG_CHUNK)), _resident((SG_CHUNK, SG_GROUPS)),
        _resident((CV_KERNEL, HALF)), _resident((1, HALF)), _resident((1, HALF)), _resident((1, HALF)),
        _resident((SC_KERNEL, HALF)),
        _resident((N_BRANCH, HALF, D_MODEL)),
        _resident((D_MODEL, D_MODEL)),
    ]
    scratch = [
        pltpu.VMEM((CV_HALO + TILE, HALF), F32),
        pltpu.VMEM((SC_HALO + TILE, HALF), F32),
        pltpu.VMEM((4, WINDOW + TILE, LANES), BF16),
        pltpu.VMEM((4, WINDOW + TILE, LANES), BF16),
        pltpu.VMEM((TILE, HALF), BF16),
        pltpu.VMEM((TILE, HALF), BF16),
    ]
    assert len(in_specs) == N_MIXER_INPUTS
    out = pl.pallas_call(
        functools.partial(_mixer_kernel, n_cast=len(cast_src)),
        grid=(batch, tiles_per_seq),
        in_specs=in_specs + src_specs,
        out_specs=[tile_spec] + dst_specs,
        out_shape=[jax.ShapeDtypeStruct(x.shape, F32)] + dst_shapes,
        scratch_shapes=scratch,
        compiler_params=pltpu.CompilerParams(
            dimension_semantics=("arbitrary", "arbitrary"),
            vmem_limit_bytes=MIXER_VMEM_LIMIT_BYTES),
        name="mixer",
    )(sinks, x, cos_t, sin_t, nm, w_in, sg_g, sg_beta, sg_w, sg_bt,
      cv_w, cv_b, cv_g, cv_beta, sc_w, w_br, w_out, *cast_src)
    return out[0], tuple(out[1:])


def _ffn_call(x, nf, w_gu, w_dn, nfinal, final_norm, tile, cast_next=(), next_layer=0):
    batch, seq, _ = x.shape
    tiles_per_seq = seq // tile
    n_steps = batch * tiles_per_seq
    tile_spec = pl.BlockSpec((None, tile, D_MODEL), lambda b, j: (b, j, 0))
    src_specs, dst_specs, dst_shapes = _cast_specs(cast_next, next_layer, tiles_per_seq, n_steps)
    out = pl.pallas_call(
        functools.partial(_ffn_kernel, final_norm=final_norm, n_cast=len(cast_next)),
        grid=(batch, tiles_per_seq),
        in_specs=[tile_spec, _resident((1, D_MODEL)), _resident((D_MODEL, 2 * D_FF)),
                  _resident((D_FF, D_MODEL)), _resident((1, D_MODEL))] + src_specs,
        out_specs=[tile_spec] + dst_specs,
        out_shape=[jax.ShapeDtypeStruct(x.shape, F32)] + dst_shapes,
        compiler_params=pltpu.CompilerParams(
            dimension_semantics=("arbitrary", "arbitrary"),
            vmem_limit_bytes=VMEM_LIMIT_BYTES),
        name="ffn",
    )(x, nf, w_gu, w_dn, nfinal, *cast_next)
    return out[0], tuple(out[1:])


def _rope_tables(seq):
    pos = jnp.arange(seq, dtype=F32)
    inv_freq = 1.0 / (ROPE_THETA ** (jnp.arange(0, HEAD_DIM, 2, dtype=F32) / HEAD_DIM))
    ang = pos[:, None] * inv_freq[None, :]
    cos = jnp.cos(ang)
    sin = jnp.sin(ang)
    reps = LANES // HEAD_DIM
    cos_t = jnp.tile(jnp.concatenate([cos, cos], axis=-1), (1, reps))
    sin_t = jnp.tile(jnp.concatenate([-sin, sin], axis=-1), (1, reps))
    return cos_t, sin_t


def kernel(x, norm_mix, w_in, sg_ln_g, sg_ln_b, sg_w, sg_b, cv_w, cv_b, cv_ln_g, cv_ln_b,
           attn_sinks, sc_w, w_branch, w_out, norm_ffn, w_gate_up, w_down, norm_final):
    depth = w_in.shape[0]
    seq = x.shape[1]
    assert x.shape[2] == D_MODEL and seq % TILE == 0 and TILE % WINDOW == 0 and seq % FFN_TILE == 0
    cos_t, sin_t = _rope_tables(seq)
    row = lambda a: a.reshape(1, -1)
    mixer_stacked = (w_in, w_branch.reshape(depth, N_BRANCH * HALF, D_MODEL), w_out)
    ffn_stacked = (w_gate_up, w_down)
    mixer_weights = tuple(w[0].astype(BF16) for w in mixer_stacked)
    for l in range(depth):
        w_in_l, w_br_l, w_out_l = mixer_weights
        x, (w_gu_l, w_dn_l) = _mixer_call(
            x, cos_t, sin_t, attn_sinks[l], row(norm_mix[l]), w_in_l,
            row(sg_ln_g[l]), row(sg_ln_b[l]), sg_w[l], sg_b[l].T,
            cv_w[l], row(cv_b[l]), row(cv_ln_g[l]), row(cv_ln_b[l]), sc_w[l],
            w_br_l.reshape(N_BRANCH, HALF, D_MODEL), w_out_l,
            cast_src=ffn_stacked, cast_layer=l)
        last = l == depth - 1
        x, mixer_weights = _ffn_call(
            x, row(norm_ffn[l]), w_gu_l, w_dn_l, row(norm_final), final_norm=last,
            tile=FFN_TILE, cast_next=() if last else mixer_stacked, next_layer=l + 1)
    return x
```

```python
import functools
import math

import jax
import jax.numpy as jnp
from jax import lax
from jax.experimental import pallas as pl
from jax.experimental.pallas import tpu as pltpu

D_MODEL = 1024
HALF = D_MODEL // 2
SG_CHUNK = 128
SG_GROUPS = 4
CV_KERNEL = 31
HEAD_DIM = 64
N_Q_HEADS = 8
N_KV_HEADS = 2
Q_WIDTH = N_Q_HEADS * HEAD_DIM
KV_WIDTH = N_KV_HEADS * HEAD_DIM
WINDOW = 128
ROPE_THETA = 10000.0
SC_KERNEL = 3
N_BRANCH = 4
D_FF = 2816
EPS = 1e-6

OFF_A = 0
OFF_B = OFF_A + 2 * HALF
OFF_Q = OFF_B + 2 * HALF
OFF_K = OFF_Q + Q_WIDTH
OFF_V = OFF_K + KV_WIDTH
OFF_D = OFF_V + KV_WIDTH
OFF_G = OFF_D + 3 * HALF
PROJ_WIDTH = OFF_G + N_BRANCH * D_MODEL

LANES = 128
SUBLANES = 8
BF16_SUBLANES = 16
TILE = 512
FFN_TILE = 1024
CV_HALO = 32
SC_HALO = SUBLANES
CONV_ROWS = 128
MASK_VALUE = -1e30
N_MIXER_INPUTS = 17
VMEM_LIMIT_BYTES = 60 * 1024 * 1024
MIXER_VMEM_LIMIT_BYTES = 62 * 1024 * 1024

F32 = jnp.float32
BF16 = jnp.bfloat16


def _rmsnorm(x, g):
    return x * lax.rsqrt(jnp.mean(x * x, axis=-1, keepdims=True) + EPS) * g


def _layernorm(x, g, b):
    mu = jnp.mean(x, axis=-1, keepdims=True)
    xc = x - mu
    var = jnp.mean(xc * xc, axis=-1, keepdims=True)
    return xc * lax.rsqrt(var + EPS) * g + b


def _dot(a, b):
    return jnp.dot(a, b, preferred_element_type=F32)


def _dot_nt(a, b):
    return lax.dot_general(a, b, (((1,), (1,)), ((), ())), preferred_element_type=F32)


def _zero_after(t):
    return t[-1:, -LANES:] * 0.0


def _causal_conv_block(buf, w_ref, ksize, halo, r0, l0, after=None):
    rows = CONV_ROWS + SUBLANES
    out = None
    for r in range(min(SUBLANES, ksize)):
        z = None
        for a in range((ksize - 1 - r) // SUBLANES + 1):
            tap = ksize - 1 - (SUBLANES * a + r)
            start = halo + r0 - SUBLANES * (a + 1)
            w_tap = w_ref[tap:tap + 1, l0:l0 + LANES]
            if after is not None:
                w_tap = w_tap + after
            term = w_tap * buf[start:start + rows, l0:l0 + LANES]
            z = term if z is None else z + term
        if r:
            z = pltpu.roll(z, r, 0)
        z = z[SUBLANES:, :]
        out = z if out is None else out + z
    return out


def _mixer_kernel(*refs, n_cast, layer):
    (sink_ref, x_ref, cos_ref, sin_ref, nm_ref, w_in_ref,
     sg_g_ref, sg_beta_ref, sg_w_ref, sg_bt_ref,
     cv_w_ref, cv_b_ref, cv_g_ref, cv_beta_ref, sc_w_ref,
     w_br_ref, w_out_ref) = refs[:N_MIXER_INPUTS]
    cast_src = refs[N_MIXER_INPUTS:N_MIXER_INPUTS + n_cast]
    o_ref = refs[N_MIXER_INPUTS + n_cast]
    cast_dst = refs[N_MIXER_INPUTS + n_cast + 1:N_MIXER_INPUTS + 2 * n_cast + 1]
    cv_buf, sc_buf, k_buf, v_buf, ya_s, yc_s = refs[N_MIXER_INPUTS + 2 * n_cast + 1:]
    j = pl.program_id(1)

    @pl.when(j == 0)
    def _():
        cv_buf[0:CV_HALO, :] = jnp.zeros((CV_HALO, HALF), F32)
        sc_buf[0:SC_HALO, :] = jnp.zeros((SC_HALO, HALF), F32)
        k_buf[:, 0:WINDOW, :] = jnp.zeros((4, WINDOW, LANES), BF16)
        v_buf[:, 0:WINDOW, :] = jnp.zeros((4, WINDOW, LANES), BF16)

    x = x_ref[...]
    xn_f32 = _rmsnorm(x, nm_ref[...])
    xn = xn_f32.astype(BF16)
    xn_half = (0.5 * xn_f32).astype(BF16)

    def proj(off, width):
        return _dot(xn, w_in_ref[:, off:off + width])

    zb_half = _dot(xn_half, w_in_ref[:, OFF_B:OFF_B + 2 * HALF])
    cv_buf[CV_HALO:CV_HALO + TILE, :] = zb_half[:, :HALF] * (jnp.tanh(zb_half[:, HALF:]) + 1.0)

    cos = cos_ref[...]
    sin = sin_ref[...]
    lane = lax.broadcasted_iota(jnp.int32, (TILE, LANES), 1)
    first_half = (lane % HEAD_DIM) < (HEAD_DIM // 2)
    low_head = lane < HEAD_DIM

    def rotary(t):
        swapped = jnp.where(first_half,
                            pltpu.roll(t, LANES - HEAD_DIM // 2, 1),
                            pltpu.roll(t, HEAD_DIM // 2, 1))
        return t * cos + swapped * sin

    kv = proj(OFF_K, 2 * KV_WIDTH)
    kr = rotary(kv[:, :KV_WIDTH]) * (HEAD_DIM ** -0.5)
    vv = kv[:, KV_WIDTH:]
    for buf, t in ((k_buf, kr), (v_buf, vv)):
        sw = pltpu.roll(t, HEAD_DIM, 1)
        buf[0, WINDOW:WINDOW + TILE, :] = jnp.where(low_head, t, 0.0).astype(BF16)
        buf[1, WINDOW:WINDOW + TILE, :] = jnp.where(low_head, 0.0, sw).astype(BF16)
        buf[2, WINDOW:WINDOW + TILE, :] = jnp.where(low_head, sw, 0.0).astype(BF16)
        buf[3, WINDOW:WINDOW + TILE, :] = jnp.where(low_head, 0.0, t).astype(BF16)

    qz = proj(OFF_Q, Q_WIDTH)
    qi = lax.broadcasted_iota(jnp.int32, (WINDOW, 2 * WINDOW), 0)
    kj = lax.broadcasted_iota(jnp.int32, (WINDOW, 2 * WINDOW), 1)
    band = (kj > qi) & (kj <= qi + WINDOW)
    first_lo = jnp.where(j == 0, WINDOW, 0)
    bias_rest = jnp.where(band, 0.0, MASK_VALUE)
    bias_first = jnp.where(band & (kj >= first_lo), 0.0, MASK_VALUE)
    n_chunks = TILE // WINDOW
    bias2_first = jnp.concatenate([bias_first, bias_first], axis=1)
    bias2_rest = jnp.concatenate([bias_rest, bias_rest], axis=1)
    bias_tile = jnp.concatenate([bias2_first] + [bias2_rest] * (n_chunks - 1), axis=0)
    low_head_t = lax.broadcasted_iota(jnp.int32, (TILE, LANES), 1) < HEAD_DIM

    def attention_group(g, after):
        h = (g * 2) // (N_Q_HEADS // N_KV_HEADS)
        qg = rotary(qz[:, g * LANES:(g + 1) * LANES])
        if after is not None:
            qg = qg + after
        qg = qg.astype(BF16)
        windows = [slice(c * WINDOW, c * WINDOW + 2 * WINDOW) for c in range(n_chunks)]
        s = jnp.concatenate(
            [jnp.concatenate([_dot_nt(qg[c * WINDOW:(c + 1) * WINDOW, :], k_buf[2 * h + side, win, :])
                              for side in range(2)], axis=1)
             for c, win in enumerate(windows)], axis=0) + bias_tile
        probs = []
        inv = []
        for side in range(2):
            s_side = s[:, side * 2 * WINDOW:(side + 1) * 2 * WINDOW]
            sink = sink_ref[layer, 2 * g + side]
            m = jnp.maximum(jnp.max(s_side, axis=-1, keepdims=True), sink)
            p = jnp.exp(s_side - m)
            den = jnp.sum(p, axis=-1, keepdims=True) + jnp.exp(sink - m)
            inv.append(1.0 / den)
            probs.append(p.astype(BF16))
        out = jnp.concatenate(
            [_dot(probs[0][c * WINDOW:(c + 1) * WINDOW, :], v_buf[2 * h, win, :])
             + _dot(probs[1][c * WINDOW:(c + 1) * WINDOW, :], v_buf[2 * h + 1, win, :])
             for c, win in enumerate(windows)], axis=0)
        out = out * jnp.where(low_head_t, inv[0], inv[1])
        yc_s[:, g * LANES:(g + 1) * LANES] = out.astype(BF16)

    assert HALF // LANES == N_BRANCH and Q_WIDTH // LANES == N_BRANCH
    conv_cols = []
    gates = []
    after = None
    for n in range(N_BRANCH):
        half_z = _dot(xn_half, w_in_ref[:, OFF_G + n * D_MODEL:OFF_G + (n + 1) * D_MODEL])
        t = jnp.tanh(half_z)
        gates.append(t + 1.0)
        conv_cols.append(jnp.concatenate(
            [_causal_conv_block(cv_buf, cv_w_ref, CV_KERNEL, CV_HALO, r0, n * LANES, after)
             for r0 in range(0, TILE, CONV_ROWS)], axis=0))
        attention_group(n, after)
        after = _zero_after(t)
    for buf in (k_buf, v_buf):
        buf[:, 0:WINDOW, :] = buf[:, TILE:TILE + WINDOW, :]
    conv = jnp.concatenate(conv_cols, axis=1)
    cv_buf[0:CV_HALO, :] = cv_buf[TILE:TILE + CV_HALO, :]

    za_half = _dot(xn_half, w_in_ref[:, OFF_A:OFF_A + 2 * HALF])
    yb_half = _layernorm(conv + (cv_b_ref[...] + after[:, 0:1]),
                         0.5 * cv_g_ref[...], 0.5 * cv_beta_ref[...])
    yb = (yb_half * (jnp.tanh(yb_half) + 1.0)).astype(BF16)

    zd = proj(OFF_D, 3 * HALF)
    ga = za_half * (1.0 + lax.erf(za_half * math.sqrt(2.0)))
    u = ga[:, :HALF]
    vln = _layernorm(ga[:, HALF:], sg_g_ref[...], sg_beta_ref[...]).astype(BF16)

    row = lax.broadcasted_iota(jnp.int32, (SG_CHUNK, SG_CHUNK), 0)
    col = lax.broadcasted_iota(jnp.int32, (SG_CHUNK, SG_CHUNK), 1)
    gw = SG_CHUNK
    for g in range(SG_GROUPS):
        w_g = jnp.where(row >= col, sg_w_ref[g], 0.0).astype(BF16)
        b_g = sg_bt_ref[:, g:g + 1]
        v_wide = jnp.concatenate(
            [vln[r0:r0 + SG_CHUNK, g * gw:(g + 1) * gw] for r0 in range(0, TILE, SG_CHUNK)], axis=1)
        mixed_wide = _dot(w_g, v_wide) + b_g
        mixed = jnp.concatenate(
            [mixed_wide[:, c * gw:(c + 1) * gw] for c in range(TILE // SG_CHUNK)], axis=0)
        ya_s[:, g * gw:(g + 1) * gw] = (u[:, g * gw:(g + 1) * gw] * mixed).astype(BF16)

    sc_buf[SC_HALO:SC_HALO + TILE, :] = zd[:, HALF:2 * HALF] * zd[:, 2 * HALF:]
    conv = jnp.concatenate(
        [jnp.concatenate([_causal_conv_block(sc_buf, sc_w_ref, SC_KERNEL, SC_HALO, r0, l0)
                          for r0 in range(0, TILE, CONV_ROWS)], axis=0)
         for l0 in range(0, HALF, LANES)], axis=1)
    sc_buf[0:SC_HALO, :] = sc_buf[TILE:TILE + SC_HALO, :]
    yd = (zd[:, :HALF] * conv).astype(BF16)

    merged = (gates[1] * _dot(yb, w_br_ref[1]) + gates[2] * _dot(yc_s[...], w_br_ref[2])
              + gates[0] * _dot(ya_s[...], w_br_ref[0]) + gates[3] * _dot(yd, w_br_ref[3]))
    o_ref[...] = x + 0.5 * _dot(merged.astype(BF16), w_out_ref[...])
    for src, dst in zip(cast_src, cast_dst):
        dst[...] = src[...].astype(BF16)


def _ffn_kernel(*refs, final_norm, n_cast):
    x_ref, nf_ref, w_gu_ref, w_dn_ref, nfinal_ref = refs[:5]
    cast_src = refs[5:5 + n_cast]
    o_ref = refs[5 + n_cast]
    cast_dst = refs[6 + n_cast:]
    x = x_ref[...]
    hn_f32 = _rmsnorm(x, nf_ref[...])
    gate_half = _dot((0.5 * hn_f32).astype(BF16), w_gu_ref[:, :D_FF])
    up = _dot(hn_f32.astype(BF16), w_gu_ref[:, D_FF:])
    h = (gate_half * (jnp.tanh(gate_half) + 1.0) * up).astype(BF16)
    y = x + _dot(h, w_dn_ref[...])
    if final_norm:
        y = _rmsnorm(y, nfinal_ref[...])
    o_ref[...] = y
    for src, dst in zip(cast_src, cast_dst):
        dst[...] = src[...].astype(BF16)


def _resident(shape, layer=None):
    if layer is None:
        return pl.BlockSpec(shape, lambda b, j: (0,) * len(shape), pipeline_mode=pl.Buffered(1))
    return pl.BlockSpec((None,) + shape, lambda b, j: (layer,) + (0,) * len(shape),
                        pipeline_mode=pl.Buffered(1))


def _cast_rows(total_rows, n_steps):
    for rows in range(BF16_SUBLANES, total_rows + 1, BF16_SUBLANES):
        if total_rows % rows == 0 and rows * n_steps >= total_rows:
            return rows
    raise ValueError((total_rows, n_steps))


def _cast_specs(cast_src, layer, tiles_per_seq, n_steps):
    src_specs, dst_specs, dst_shapes = [], [], []
    for w in cast_src:
        _, total_rows, cols = w.shape
        rows = _cast_rows(total_rows, n_steps)
        last = total_rows // rows - 1

        def block(b, j, last=last):
            return jnp.minimum(b * tiles_per_seq + j, last)
        src_specs.append(pl.BlockSpec((None, rows, cols), lambda b, j, block=block: (layer, block(b, j), 0)))
        dst_specs.append(pl.BlockSpec((rows, cols), lambda b, j, block=block: (block(b, j), 0)))
        dst_shapes.append(jax.ShapeDtypeStruct((total_rows, cols), BF16))
    return src_specs, dst_specs, dst_shapes


def _mixer_call(x, cos_t, sin_t, sinks, nm, w_in, sg_g, sg_beta, sg_w, sg_bt,
                cv_w, cv_b, cv_g, cv_beta, sc_w, w_br, w_out, layer, cast_src=()):
    batch, seq, _ = x.shape
    tiles_per_seq = seq // TILE
    src_specs, dst_specs, dst_shapes = _cast_specs(cast_src, layer, tiles_per_seq, batch * tiles_per_seq)
    tile_spec = pl.BlockSpec((None, TILE, D_MODEL), lambda b, j: (b, j, 0))
    rope_spec = pl.BlockSpec((TILE, LANES), lambda b, j: (j, 0))
    in_specs = [
        pl.BlockSpec(memory_space=pltpu.SMEM),
        tile_spec, rope_spec, rope_spec,
        _resident((1, D_MODEL), layer),
        _resident((D_MODEL, PROJ_WIDTH)),
        _resident((1, HALF), layer), _resident((1, HALF), layer),
        _resident((SG_GROUPS, SG_CHUNK, SG_CHUNK), layer), _resident((SG_CHUNK, SG_GROUPS), layer),
        _resident((CV_KERNEL, HALF), layer), _resident((1, HALF), layer), _resident((1, HALF), layer),
        _resident((1, HALF), layer),
        _resident((SC_KERNEL, HALF), layer),
        _resident((N_BRANCH, HALF, D_MODEL)),
        _resident((D_MODEL, D_MODEL)),
    ]
    scratch = [
        pltpu.VMEM((CV_HALO + TILE, HALF), F32),
        pltpu.VMEM((SC_HALO + TILE, HALF), F32),
        pltpu.VMEM((4, WINDOW + TILE, LANES), BF16),
        pltpu.VMEM((4, WINDOW + TILE, LANES), BF16),
        pltpu.VMEM((TILE, HALF), BF16),
        pltpu.VMEM((TILE, HALF), BF16),
    ]
    assert len(in_specs) == N_MIXER_INPUTS
    out = pl.pallas_call(
        functools.partial(_mixer_kernel, n_cast=len(cast_src), layer=layer),
        grid=(batch, tiles_per_seq),
        in_specs=in_specs + src_specs,
        out_specs=[tile_spec] + dst_specs,
        out_shape=[jax.ShapeDtypeStruct(x.shape, F32)] + dst_shapes,
        scratch_shapes=scratch,
        compiler_params=pltpu.CompilerParams(
            dimension_semantics=("arbitrary", "arbitrary"),
            vmem_limit_bytes=MIXER_VMEM_LIMIT_BYTES),
        name="mixer",
    )(sinks, x, cos_t, sin_t, nm, w_in, sg_g, sg_beta, sg_w, sg_bt,
      cv_w, cv_b, cv_g, cv_beta, sc_w, w_br, w_out, *cast_src)
    return out[0], tuple(out[1:])


def _ffn_call(x, nf, w_gu, w_dn, nfinal, final_norm, tile, layer, cast_next=(), next_layer=0):
    batch, seq, _ = x.shape
    tiles_per_seq = seq // tile
    n_steps = batch * tiles_per_seq
    tile_spec = pl.BlockSpec((None, tile, D_MODEL), lambda b, j: (b, j, 0))
    src_specs, dst_specs, dst_shapes = _cast_specs(cast_next, next_layer, tiles_per_seq, n_steps)
    out = pl.pallas_call(
        functools.partial(_ffn_kernel, final_norm=final_norm, n_cast=len(cast_next)),
        grid=(batch, tiles_per_seq),
        in_specs=[tile_spec, _resident((1, D_MODEL), layer), _resident((D_MODEL, 2 * D_FF)),
                  _resident((D_FF, D_MODEL)), _resident((1, D_MODEL))] + src_specs,
        out_specs=[tile_spec] + dst_specs,
        out_shape=[jax.ShapeDtypeStruct(x.shape, F32)] + dst_shapes,
        compiler_params=pltpu.CompilerParams(
            dimension_semantics=("arbitrary", "arbitrary"),
            vmem_limit_bytes=VMEM_LIMIT_BYTES),
        name="ffn",
    )(x, nf, w_gu, w_dn, nfinal, *cast_next)
    return out[0], tuple(out[1:])


def _rope_tables(seq):
    pos = jnp.arange(seq, dtype=F32)
    inv_freq = 1.0 / (ROPE_THETA ** (jnp.arange(0, HEAD_DIM, 2, dtype=F32) / HEAD_DIM))
    ang = pos[:, None] * inv_freq[None, :]
    cos = jnp.cos(ang)
    sin = jnp.sin(ang)
    reps = LANES // HEAD_DIM
    cos_t = jnp.tile(jnp.concatenate([cos, cos], axis=-1), (1, reps))
    sin_t = jnp.tile(jnp.concatenate([-sin, sin], axis=-1), (1, reps))
    return cos_t, sin_t


def kernel(x, norm_mix, w_in, sg_ln_g, sg_ln_b, sg_w, sg_b, cv_w, cv_b, cv_ln_g, cv_ln_b,
           attn_sinks, sc_w, w_branch, w_out, norm_ffn, w_gate_up, w_down, norm_final):
    depth = w_in.shape[0]
    seq = x.shape[1]
    assert x.shape[2] == D_MODEL and seq % TILE == 0 and TILE % WINDOW == 0 and seq % FFN_TILE == 0
    cos_t, sin_t = _rope_tables(seq)
    row = lambda a: a.reshape(1, -1)
    mixer_stacked = (w_in, w_branch.reshape(depth, N_BRANCH * HALF, D_MODEL), w_out)
    ffn_stacked = (w_gate_up, w_down)
    mixer_weights = tuple(w[0].astype(BF16) for w in mixer_stacked)
    rows = lambda a: a.reshape(depth, 1, -1)
    small = (rows(sg_ln_g), rows(sg_ln_b), sg_w, sg_b.transpose(0, 2, 1),
             cv_w, rows(cv_b), rows(cv_ln_g), rows(cv_ln_b), sc_w)
    norm_mix_rows, norm_ffn_rows = rows(norm_mix), rows(norm_ffn)
    for l in range(depth):
        w_in_l, w_br_l, w_out_l = mixer_weights
        x, (w_gu_l, w_dn_l) = _mixer_call(
            x, cos_t, sin_t, attn_sinks, norm_mix_rows, w_in_l, *small,
            w_br_l.reshape(N_BRANCH, HALF, D_MODEL), w_out_l, layer=l, cast_src=ffn_stacked)
        last = l == depth - 1
        x, mixer_weights = _ffn_call(
            x, norm_ffn_rows, w_gu_l, w_dn_l, row(norm_final), final_norm=last,
            tile=FFN_TILE, layer=l, cast_next=() if last else mixer_stacked, next_layer=l + 1)
    return x
```

```python
import functools
import math

import jax
import jax.numpy as jnp
from jax import lax
from jax.experimental import pallas as pl
from jax.experimental.pallas import tpu as pltpu

D_MODEL = 1024
HALF = D_MODEL // 2
SG_CHUNK = 128
SG_GROUPS = 4
CV_KERNEL = 31
HEAD_DIM = 64
N_Q_HEADS = 8
N_KV_HEADS = 2
Q_WIDTH = N_Q_HEADS * HEAD_DIM
KV_WIDTH = N_KV_HEADS * HEAD_DIM
WINDOW = 128
ROPE_THETA = 10000.0
SC_KERNEL = 3
N_BRANCH = 4
D_FF = 2816
EPS = 1e-6

OFF_A = 0
OFF_B = OFF_A + 2 * HALF
OFF_Q = OFF_B + 2 * HALF
OFF_K = OFF_Q + Q_WIDTH
OFF_V = OFF_K + KV_WIDTH
OFF_D = OFF_V + KV_WIDTH
OFF_G = OFF_D + 3 * HALF
PROJ_WIDTH = OFF_G + N_BRANCH * D_MODEL

LANES = 128
SUBLANES = 8
BF16_SUBLANES = 16
TILE = 512
FFN_TILE = 1024
CV_HALO = 32
SC_HALO = SUBLANES
CONV_ROWS = 128
MASK_VALUE = -1e30
N_MIXER_INPUTS = 17
VMEM_LIMIT_BYTES = 60 * 1024 * 1024
MIXER_VMEM_LIMIT_BYTES = 62 * 1024 * 1024

F32 = jnp.float32
BF16 = jnp.bfloat16


def _rmsnorm(x, g):
    return x * lax.rsqrt(jnp.mean(x * x, axis=-1, keepdims=True) + EPS) * g


def _layernorm(x, g, b):
    mu = jnp.mean(x, axis=-1, keepdims=True)
    xc = x - mu
    var = jnp.mean(xc * xc, axis=-1, keepdims=True)
    return xc * lax.rsqrt(var + EPS) * g + b


def _dot(a, b):
    return jnp.dot(a, b, preferred_element_type=F32)


def _dot_nt(a, b):
    return lax.dot_general(a, b, (((1,), (1,)), ((), ())), preferred_element_type=F32)


def _zero_after(t):
    return t[-1:, -LANES:] * 0.0


def _causal_conv_block(buf, w_ref, ksize, halo, r0, l0, after=None):
    rows = CONV_ROWS + SUBLANES
    out = None
    for r in range(min(SUBLANES, ksize)):
        z = None
        for a in range((ksize - 1 - r) // SUBLANES + 1):
            tap = ksize - 1 - (SUBLANES * a + r)
            start = halo + r0 - SUBLANES * (a + 1)
            w_tap = w_ref[tap:tap + 1, l0:l0 + LANES]
            if after is not None:
                w_tap = w_tap + after
            term = w_tap * buf[start:start + rows, l0:l0 + LANES]
            z = term if z is None else z + term
        if r:
            z = pltpu.roll(z, r, 0)
        z = z[SUBLANES:, :]
        out = z if out is None else out + z
    return out


def _mixer_kernel(*refs, n_cast):
    (sink_ref, x_ref, cos_ref, sin_ref, nm_ref, w_in_ref,
     sg_g_ref, sg_beta_ref, sg_w_ref, sg_bt_ref,
     cv_w_ref, cv_b_ref, cv_g_ref, cv_beta_ref, sc_w_ref,
     w_br_ref, w_out_ref) = refs[:N_MIXER_INPUTS]
    cast_src = refs[N_MIXER_INPUTS:N_MIXER_INPUTS + n_cast]
    o_ref = refs[N_MIXER_INPUTS + n_cast]
    cast_dst = refs[N_MIXER_INPUTS + n_cast + 1:N_MIXER_INPUTS + 2 * n_cast + 1]
    cv_buf, sc_buf, k_buf, v_buf, ya_s, yc_s = refs[N_MIXER_INPUTS + 2 * n_cast + 1:]
    j = pl.program_id(1)

    @pl.when(j == 0)
    def _():
        cv_buf[0:CV_HALO, :] = jnp.zeros((CV_HALO, HALF), F32)
        sc_buf[0:SC_HALO, :] = jnp.zeros((SC_HALO, HALF), F32)
        k_buf[:, 0:WINDOW, :] = jnp.zeros((4, WINDOW, LANES), BF16)
        v_buf[:, 0:WINDOW, :] = jnp.zeros((4, WINDOW, LANES), BF16)

    x = x_ref[...]
    xn_f32 = _rmsnorm(x, nm_ref[...])
    xn = xn_f32.astype(BF16)
    xn_half = (0.5 * xn_f32).astype(BF16)

    def proj(off, width):
        return _dot(xn, w_in_ref[:, off:off + width])

    zb_half = _dot(xn_half, w_in_ref[:, OFF_B:OFF_B + 2 * HALF])
    cv_buf[CV_HALO:CV_HALO + TILE, :] = zb_half[:, :HALF] * (jnp.tanh(zb_half[:, HALF:]) + 1.0)

    cos = cos_ref[...]
    sin = sin_ref[...]
    lane = lax.broadcasted_iota(jnp.int32, (TILE, LANES), 1)
    first_half = (lane % HEAD_DIM) < (HEAD_DIM // 2)
    low_head = lane < HEAD_DIM

    def rotary(t):
        swapped = jnp.where(first_half,
                            pltpu.roll(t, LANES - HEAD_DIM // 2, 1),
                            pltpu.roll(t, HEAD_DIM // 2, 1))
        return t * cos + swapped * sin

    kv = proj(OFF_K, 2 * KV_WIDTH)
    kr = rotary(kv[:, :KV_WIDTH]) * (HEAD_DIM ** -0.5)
    vv = kv[:, KV_WIDTH:]
    for buf, t in ((k_buf, kr), (v_buf, vv)):
        sw = pltpu.roll(t, HEAD_DIM, 1)
        buf[0, WINDOW:WINDOW + TILE, :] = jnp.where(low_head, t, 0.0).astype(BF16)
        buf[1, WINDOW:WINDOW + TILE, :] = jnp.where(low_head, 0.0, sw).astype(BF16)
        buf[2, WINDOW:WINDOW + TILE, :] = jnp.where(low_head, sw, 0.0).astype(BF16)
        buf[3, WINDOW:WINDOW + TILE, :] = jnp.where(low_head, 0.0, t).astype(BF16)

    qz = proj(OFF_Q, Q_WIDTH)
    qi = lax.broadcasted_iota(jnp.int32, (WINDOW, 2 * WINDOW), 0)
    kj = lax.broadcasted_iota(jnp.int32, (WINDOW, 2 * WINDOW), 1)
    band = (kj > qi) & (kj <= qi + WINDOW)
    first_lo = jnp.where(j == 0, WINDOW, 0)
    bias_rest = jnp.where(band, 0.0, MASK_VALUE)
    bias_first = jnp.where(band & (kj >= first_lo), 0.0, MASK_VALUE)
    n_chunks = TILE // WINDOW
    bias2_first = jnp.concatenate([bias_first, bias_first], axis=1)
    bias2_rest = jnp.concatenate([bias_rest, bias_rest], axis=1)
    bias_tile = jnp.concatenate([bias2_first] + [bias2_rest] * (n_chunks - 1), axis=0)
    low_head_t = lax.broadcasted_iota(jnp.int32, (TILE, LANES), 1) < HEAD_DIM

    def attention_group(g, after):
        h = (g * 2) // (N_Q_HEADS // N_KV_HEADS)
        qg = rotary(qz[:, g * LANES:(g + 1) * LANES])
        if after is not None:
            qg = qg + after
        qg = qg.astype(BF16)
        windows = [slice(c * WINDOW, c * WINDOW + 2 * WINDOW) for c in range(n_chunks)]
        s = jnp.concatenate(
            [jnp.concatenate([_dot_nt(qg[c * WINDOW:(c + 1) * WINDOW, :], k_buf[2 * h + side, win, :])
                              for side in range(2)], axis=1)
             for c, win in enumerate(windows)], axis=0) + bias_tile
        probs = []
        inv = []
        for side in range(2):
            s_side = s[:, side * 2 * WINDOW:(side + 1) * 2 * WINDOW]
            sink = sink_ref[2 * g + side]
            m = jnp.maximum(jnp.max(s_side, axis=-1, keepdims=True), sink)
            p = jnp.exp(s_side - m)
            den = jnp.sum(p, axis=-1, keepdims=True) + jnp.exp(sink - m)
            inv.append(1.0 / den)
            probs.append(p.astype(BF16))
        out = jnp.concatenate(
            [_dot(probs[0][c * WINDOW:(c + 1) * WINDOW, :], v_buf[2 * h, win, :])
             + _dot(probs[1][c * WINDOW:(c + 1) * WINDOW, :], v_buf[2 * h + 1, win, :])
             for c, win in enumerate(windows)], axis=0)
        out = out * jnp.where(low_head_t, inv[0], inv[1])
        yc_s[:, g * LANES:(g + 1) * LANES] = out.astype(BF16)

    assert HALF // LANES == N_BRANCH and Q_WIDTH // LANES == N_BRANCH
    conv_cols = []
    gates = []
    after = None
    for n in range(N_BRANCH):
        half_z = _dot(xn_half, w_in_ref[:, OFF_G + n * D_MODEL:OFF_G + (n + 1) * D_MODEL])
        t = jnp.tanh(half_z)
        gates.append(t + 1.0)
        conv_cols.append(jnp.concatenate(
            [_causal_conv_block(cv_buf, cv_w_ref, CV_KERNEL, CV_HALO, r0, n * LANES, after)
             for r0 in range(0, TILE, CONV_ROWS)], axis=0))
        attention_group(n, after)
        after = _zero_after(t)
    for buf in (k_buf, v_buf):
        buf[:, 0:WINDOW, :] = buf[:, TILE:TILE + WINDOW, :]
    conv = jnp.concatenate(conv_cols, axis=1)
    cv_buf[0:CV_HALO, :] = cv_buf[TILE:TILE + CV_HALO, :]

    za_half = _dot(xn_half, w_in_ref[:, OFF_A:OFF_A + 2 * HALF])
    yb_half = _layernorm(conv + (cv_b_ref[...] + after[:, 0:1]),
                         0.5 * cv_g_ref[...], 0.5 * cv_beta_ref[...])
    yb = (yb_half * (jnp.tanh(yb_half) + 1.0)).astype(BF16)

    zd = proj(OFF_D, 3 * HALF)
    ga = za_half * (1.0 + lax.erf(za_half * math.sqrt(2.0)))
    u = ga[:, :HALF]
    vln = _layernorm(ga[:, HALF:], sg_g_ref[...], sg_beta_ref[...]).astype(BF16)

    row = lax.broadcasted_iota(jnp.int32, (SG_CHUNK, SG_CHUNK), 0)
    col = lax.broadcasted_iota(jnp.int32, (SG_CHUNK, SG_CHUNK), 1)
    gw = SG_CHUNK
    for g in range(SG_GROUPS):
        w_g = jnp.where(row >= col, sg_w_ref[g], 0.0).astype(BF16)
        b_g = sg_bt_ref[:, g:g + 1]
        v_wide = jnp.concatenate(
            [vln[r0:r0 + SG_CHUNK, g * gw:(g + 1) * gw] for r0 in range(0, TILE, SG_CHUNK)], axis=1)
        mixed_wide = _dot(w_g, v_wide) + b_g
        mixed = jnp.concatenate(
            [mixed_wide[:, c * gw:(c + 1) * gw] for c in range(TILE // SG_CHUNK)], axis=0)
        ya_s[:, g * gw:(g + 1) * gw] = (u[:, g * gw:(g + 1) * gw] * mixed).astype(BF16)

    sc_buf[SC_HALO:SC_HALO + TILE, :] = zd[:, HALF:2 * HALF] * zd[:, 2 * HALF:]
    conv = jnp.concatenate(
        [jnp.concatenate([_causal_conv_block(sc_buf, sc_w_ref, SC_KERNEL, SC_HALO, r0, l0)
                          for r0 in range(0, TILE, CONV_ROWS)], axis=0)
         for l0 in range(0, HALF, LANES)], axis=1)
    sc_buf[0:SC_HALO, :] = sc_buf[TILE:TILE + SC_HALO, :]
    yd = (zd[:, :HALF] * conv).astype(BF16)

    merged = (gates[1] * _dot(yb, w_br_ref[1]) + gates[2] * _dot(yc_s[...], w_br_ref[2])
              + gates[0] * _dot(ya_s[...], w_br_ref[0]) + gates[3] * _dot(yd, w_br_ref[3]))
    o_ref[...] = x + 0.5 * _dot(merged.astype(BF16), w_out_ref[...])
    for src, dst in zip(cast_src, cast_dst):
        dst[...] = src[...].astype(BF16)


def _ffn_kernel(*refs, final_norm, n_cast):
    x_ref, nf_ref, w_gu_ref, w_dn_ref, nfinal_ref = refs[:5]
    cast_src = refs[5:5 + n_cast]
    o_ref = refs[5 + n_cast]
    cast_dst = refs[6 + n_cast:]
    x = x_ref[...]
    hn_f32 = _rmsnorm(x, nf_ref[...])
    gate_half = _dot((0.5 * hn_f32).astype(BF16), w_gu_ref[:, :D_FF])
    up = _dot(hn_f32.astype(BF16), w_gu_ref[:, D_FF:])
    h = (gate_half * (jnp.tanh(gate_half) + 1.0) * up).astype(BF16)
    y = x + _dot(h, w_dn_ref[...])
    if final_norm:
        y = _rmsnorm(y, nfinal_ref[...])
    o_ref[...] = y
    for src, dst in zip(cast_src, cast_dst):
        dst[...] = src[...].astype(BF16)


def _resident(shape):
    return pl.BlockSpec(shape, lambda b, j: (0,) * len(shape), pipeline_mode=pl.Buffered(1))


def _cast_rows(total_rows, n_steps):
    for rows in range(BF16_SUBLANES, total_rows + 1, BF16_SUBLANES):
        if total_rows % rows == 0 and rows * n_steps >= total_rows:
            return rows
    raise ValueError((total_rows, n_steps))


def _cast_specs(cast_src, layer, tiles_per_seq, n_steps):
    src_specs, dst_specs, dst_shapes = [], [], []
    for w in cast_src:
        _, total_rows, cols = w.shape
        rows = _cast_rows(total_rows, n_steps)
        last = total_rows // rows - 1

        def block(b, j, last=last):
            return jnp.minimum(b * tiles_per_seq + j, last)
        src_specs.append(pl.BlockSpec((None, rows, cols), lambda b, j, block=block: (layer, block(b, j), 0)))
        dst_specs.append(pl.BlockSpec((rows, cols), lambda b, j, block=block: (block(b, j), 0)))
        dst_shapes.append(jax.ShapeDtypeStruct((total_rows, cols), BF16))
    return src_specs, dst_specs, dst_shapes


def _mixer_call(x, cos_t, sin_t, sinks, nm, w_in, sg_g, sg_beta, sg_w, sg_bt,
                cv_w, cv_b, cv_g, cv_beta, sc_w, w_br, w_out, cast_src=(), cast_layer=0):
    batch, seq, _ = x.shape
    tiles_per_seq = seq // TILE
    src_specs, dst_specs, dst_shapes = _cast_specs(cast_src, cast_layer, tiles_per_seq, batch * tiles_per_seq)
    tile_spec = pl.BlockSpec((None, TILE, D_MODEL), lambda b, j: (b, j, 0))
    rope_spec = pl.BlockSpec((TILE, LANES), lambda b, j: (j, 0))
    in_specs = [
        pl.BlockSpec(memory_space=pltpu.SMEM),
        tile_spec, rope_spec, rope_spec,
        _resident((1, D_MODEL)),
        _resident((D_MODEL, PROJ_WIDTH)),
        _resident((1, HALF)), _resident((1, HALF)),
        _resident((SG_GROUPS, SG_CHUNK, SG_CHUNK)), _resident((SG_CHUNK, SG_GROUPS)),
        _resident((CV_KERNEL, HALF)), _resident((1, HALF)), _resident((1, HALF)), _resident((1, HALF)),
        _resident((SC_KERNEL, HALF)),
        _resident((N_BRANCH, HALF, D_MODEL)),
        _resident((D_MODEL, D_MODEL)),
    ]
    scratch = [
        pltpu.VMEM((CV_HALO + TILE, HALF), F32),
        pltpu.VMEM((SC_HALO + TILE, HALF), F32),
        pltpu.VMEM((4, WINDOW + TILE, LANES), BF16),
        pltpu.VMEM((4, WINDOW + TILE, LANES), BF16),
        pltpu.VMEM((TILE, HALF), BF16),
        pltpu.VMEM((TILE, HALF), BF16),
    ]
    assert len(in_specs) == N_MIXER_INPUTS
    out = pl.pallas_call(
        functools.partial(_mixer_kernel, n_cast=len(cast_src)),
        grid=(batch, tiles_per_seq),
        in_specs=in_specs + src_specs,
        out_specs=[tile_spec] + dst_specs,
        out_shape=[jax.ShapeDtypeStruct(x.shape, F32)] + dst_shapes,
        scratch_shapes=scratch,
        compiler_params=pltpu.CompilerParams(
            dimension_semantics=("arbitrary", "arbitrary"),
            vmem_limit_bytes=MIXER_VMEM_LIMIT_BYTES),
        name="mixer",
    )(sinks, x, cos_t, sin_t, nm, w_in, sg_g, sg_beta, sg_w, sg_bt,
      cv_w, cv_b, cv_g, cv_beta, sc_w, w_br, w_out, *cast_src)
    return out[0], tuple(out[1:])


def _ffn_call(x, nf, w_gu, w_dn, nfinal, final_norm, tile, cast_next=(), next_layer=0):
    batch, seq, _ = x.shape
    tiles_per_seq = seq // tile
    n_steps = batch * tiles_per_seq
    tile_spec = pl.BlockSpec((None, tile, D_MODEL), lambda b, j: (b, j, 0))
    src_specs, dst_specs, dst_shapes = _cast_specs(cast_next, next_layer, tiles_per_seq, n_steps)
    out = pl.pallas_call(
        functools.partial(_ffn_kernel, final_norm=final_norm, n_cast=len(cast_next)),
        grid=(batch, tiles_per_seq),
        in_specs=[tile_spec, _resident((1, D_MODEL)), _resident((D_MODEL, 2 * D_FF)),
                  _resident((D_FF, D_MODEL)), _resident((1, D_MODEL))] + src_specs,
        out_specs=[tile_spec] + dst_specs,
        out_shape=[jax.ShapeDtypeStruct(x.shape, F32)] + dst_shapes,
        compiler_params=pltpu.CompilerParams(
            dimension_semantics=("parallel", "arbitrary"),
            vmem_limit_bytes=VMEM_LIMIT_BYTES),
        name="ffn",
    )(x, nf, w_gu, w_dn, nfinal, *cast_next)
    return out[0], tuple(out[1:])


def _rope_tables(seq):
    pos = jnp.arange(seq, dtype=F32)
    inv_freq = 1.0 / (ROPE_THETA ** (jnp.arange(0, HEAD_DIM, 2, dtype=F32) / HEAD_DIM))
    ang = pos[:, None] * inv_freq[None, :]
    cos = jnp.cos(ang)
    sin = jnp.sin(ang)
    reps = LANES // HEAD_DIM
    cos_t = jnp.tile(jnp.concatenate([cos, cos], axis=-1), (1, reps))
    sin_t = jnp.tile(jnp.concatenate([-sin, sin], axis=-1), (1, reps))
    return cos_t, sin_t


def kernel(x, norm_mix, w_in, sg_ln_g, sg_ln_b, sg_w, sg_b, cv_w, cv_b, cv_ln_g, cv_ln_b,
           attn_sinks, sc_w, w_branch, w_out, norm_ffn, w_gate_up, w_down, norm_final):
    depth = w_in.shape[0]
    seq = x.shape[1]
    assert x.shape[2] == D_MODEL and seq % TILE == 0 and TILE % WINDOW == 0 and seq % FFN_TILE == 0
    cos_t, sin_t = _rope_tables(seq)
    row = lambda a: a.reshape(1, -1)
    mixer_stacked = (w_in, w_branch.reshape(depth, N_BRANCH * HALF, D_MODEL), w_out)
    ffn_stacked = (w_gate_up, w_down)
    mixer_weights = tuple(w[0].astype(BF16) for w in mixer_stacked)
    for l in range(depth):
        w_in_l, w_br_l, w_out_l = mixer_weights
        x, (w_gu_l, w_dn_l) = _mixer_call(
            x, cos_t, sin_t, attn_sinks[l], row(norm_mix[l]), w_in_l,
            row(sg_ln_g[l]), row(sg_ln_b[l]), sg_w[l], sg_b[l].T,
            cv_w[l], row(cv_b[l]), row(cv_ln_g[l]), row(cv_ln_b[l]), sc_w[l],
            w_br_l.reshape(N_BRANCH, HALF, D_MODEL), w_out_l,
            cast_src=ffn_stacked, cast_layer=l)
        last = l == depth - 1
        x, mixer_weights = _ffn_call(
            x, row(norm_ffn[l]), w_gu_l, w_dn_l, row(norm_final), final_norm=last,
            tile=FFN_TILE, cast_next=() if last else mixer_stacked, next_layer=l + 1)
    return x
```
